```python
import jax, jax.numpy as jnp
from jax import lax
import numpy as np

D_MODEL = 4096
BATCH = 4
SEQ = 2048
DEPTH = 4

N_MIXERS = 3
D_FF = 5632
RET_HEADS = 16
RET_DK = D_MODEL // RET_HEADS
RET_DV = D_MODEL // RET_HEADS
RET_CHUNK = 128
SG_GROUPS = 16
SG_CHUNK = 128
MOBA_HEADS = 32
MOBA_DH = D_MODEL // MOBA_HEADS
MOBA_BLOCK = 256
MOBA_TOPK = 3
MOBA_QBLOCK = 8
ROPE_THETA = 10000.0
EPS = 1e-6

kernel_name = "hybrid_retention_sgu_moba_macaron"


def _n_layers_of(m):
    return len(range(m, DEPTH, N_MIXERS))


def rms_norm(x, g):
    xf = x.astype(jnp.float32)
    y = xf * lax.rsqrt(jnp.mean(xf * xf, axis=-1, keepdims=True) + EPS)
    return (y * g.astype(jnp.float32)).astype(x.dtype)


def layer_norm(x, g):
    xf = x.astype(jnp.float32)
    mu = jnp.mean(xf, axis=-1, keepdims=True)
    var = jnp.mean(jnp.square(xf - mu), axis=-1, keepdims=True)
    return ((xf - mu) * lax.rsqrt(var + EPS) * g.astype(jnp.float32)).astype(x.dtype)


def swiglu(x, w_in, w_out):
    a, b = jnp.split(x @ w_in, 2, axis=-1)
    return (jax.nn.silu(a) * b) @ w_out


def rotary(t, positions, inv_freq):
    ang = positions[:, None, :, None].astype(jnp.float32) * inv_freq
    cos = jnp.cos(ang).astype(t.dtype)
    sin = jnp.sin(ang).astype(t.dtype)
    t1, t2 = jnp.split(t, 2, axis=-1)
    return jnp.concatenate([t1 * cos - t2 * sin, t1 * sin + t2 * cos], axis=-1)


def retention_mixer(xn, positions, w_in, gn_g, w_out):
    B_, S, D = xn.shape
    H, dk, dv, C = RET_HEADS, RET_DK, RET_DV, RET_CHUNK
    N = S // C
    dt = xn.dtype
    q, k, v, g = jnp.split(xn @ w_in, 4, axis=-1)
    heads = lambda t, d: t.reshape(B_, S, H, d).transpose(0, 2, 1, 3)
    inv_freq = 1.0 / (10000.0 ** jnp.linspace(0.0, 1.0, dk // 2, dtype=jnp.float32))
    q = rotary(heads(q, dk), positions, inv_freq)
    k = rotary(heads(k, dk), positions, inv_freq) * (dk ** -0.5)
    v = heads(v, dv)
    log_gamma = jnp.log1p(-jnp.exp2(-5.0 - jnp.arange(H, dtype=jnp.float32)))
    idx = jnp.arange(C, dtype=jnp.float32)
    dist = idx[:, None] - idx[None, :]
    intra_decay = jnp.where(dist >= 0,
                            jnp.exp(log_gamma[:, None, None] * jnp.maximum(dist, 0.0)),
                            0.0).astype(dt)
    q_decay = jnp.exp(log_gamma[:, None] * (idx + 1.0)).astype(dt)
    k_decay = jnp.exp(log_gamma[:, None] * (C - 1.0 - idx)).astype(dt)
    chunk_decay = jnp.exp(log_gamma * C).astype(dt)
    qc = q.reshape(B_, H, N, C, dk)
    kc = k.reshape(B_, H, N, C, dk)
    vc = v.reshape(B_, H, N, C, dv)
    scores = jnp.einsum('bhncd,bhnmd->bhncm', qc, kc) * intra_decay[:, None]
    intra = jnp.einsum('bhncm,bhnme->bhnce', scores, vc)

    def step(state, inp):
        q_n, k_n, v_n = inp
        cross = jnp.einsum('bhcd,bhde->bhce', q_n * q_decay[:, :, None], state)
        state = state * chunk_decay[:, None, None] + jnp.einsum(
            'bhcd,bhce->bhde', k_n * k_decay[:, :, None], v_n)
        return state, cross

    state0 = jnp.zeros((B_, H, dk, dv), dt)
    to_chunk_major = lambda t: t.transpose(2, 0, 1, 3, 4)
    _, cross = lax.scan(step, state0, (to_chunk_major(qc), to_chunk_major(kc), to_chunk_major(vc)))
    o = (intra + cross.transpose(1, 2, 0, 3, 4)).reshape(B_, H, S, dv)
    of = o.astype(jnp.float32)
    mu = jnp.mean(of, axis=-1, keepdims=True)
    var = jnp.mean(jnp.square(of - mu), axis=-1, keepdims=True)
    o = ((of - mu) * lax.rsqrt(var + EPS)).astype(dt)
    o = o.transpose(0, 2, 1, 3).reshape(B_, S, H * dv) * gn_g
    return (jax.nn.silu(g) * o) @ w_out


def spatial_gating_mixer(xn, w_in, ln_g, w_s, b_s, w_out):
    B_, S, D = xn.shape
    G, C = SG_GROUPS, SG_CHUNK
    N = S // C
    u, v = jnp.split(jax.nn.gelu(xn @ w_in, approximate=False), 2, axis=-1)
    v = layer_norm(v, ln_g).reshape(B_, N, C, G, D // G)
    causal = jnp.tril(jnp.ones((C, C), dtype=bool))
    w_m = jnp.where(causal[None], w_s, 0.0)
    mixed = jnp.einsum('gts,bnsgc->bntgc', w_m, v) + b_s.T[None, None, :, :, None]
    return (u * mixed.reshape(B_, S, D)) @ w_out


def moba_mixer(xn, positions, w_in, w_out):
    B_, S, D = xn.shape
    H, dh, L, QB = MOBA_HEADS, MOBA_DH, MOBA_BLOCK, MOBA_QBLOCK
    nblk = -(-S // L)
    Sp = nblk * L
    n_sel = min(MOBA_TOPK, nblk - 1)
    scale = dh ** -0.5
    q, k, v = jnp.split(xn @ w_in, 3, axis=-1)
    heads = lambda t: t.reshape(B_, S, H, dh).transpose(0, 2, 1, 3)
    inv_freq = 1.0 / (ROPE_THETA ** (jnp.arange(0, dh, 2, dtype=jnp.float32) / dh))
    q = rotary(heads(q), positions, inv_freq)
    k = rotary(heads(k), positions, inv_freq)
    v = heads(v)
    pad = ((0, 0), (0, 0), (0, Sp - S), (0, 0))
    kb = jnp.pad(k, pad).reshape(B_, H, nblk, L, dh)
    vb = jnp.pad(v, pad).reshape(B_, H, nblk, L, dh)
    q_blk = jnp.arange(S) // L
    nq = S // QB
    q_steps = q.reshape(B_, H, nq, QB, dh).transpose(2, 0, 1, 3, 4)
    t0s = jnp.arange(nq, dtype=jnp.int32) * QB

    if n_sel > 0:
        k_mean = jnp.mean(kb, axis=3)
        gate = jnp.einsum('bhsd,bhnd->bhsn', q, k_mean).astype(jnp.float32)
        past = jnp.arange(nblk)[None, :] < q_blk[:, None]
        gate = jnp.where(past, gate, -jnp.inf)
        _, sel = lax.top_k(gate, n_sel)
        valid = sel < q_blk[None, None, :, None]
        sel_steps = sel.reshape(B_, H, nq, QB, n_sel).transpose(2, 0, 1, 3, 4)
        valid_steps = valid.reshape(B_, H, nq, QB, n_sel).transpose(2, 0, 1, 3, 4)
    else:
        sel_steps = jnp.zeros((nq, B_, H, QB, 1), jnp.int32)
        valid_steps = jnp.zeros((nq, B_, H, QB, 1), bool)

    bi = jnp.arange(B_)[:, None, None, None]
    hi = jnp.arange(H)[None, :, None, None]

    def attend(args):
        q_i, sel_i, valid_i, t0 = args
        own = t0 // L
        k_own = lax.dynamic_index_in_dim(kb, own, axis=2, keepdims=False)
        v_own = lax.dynamic_index_in_dim(vb, own, axis=2, keepdims=False)
        qpos = t0 + jnp.arange(QB)
        kpos = own * L + jnp.arange(L)
        s_own = jnp.einsum('bhqd,bhkd->bhqk', q_i, k_own).astype(jnp.float32) * scale
        s_own = jnp.where(kpos[None, :] <= qpos[:, None], s_own, -jnp.inf)
        if n_sel > 0:
            k_sel = kb[bi, hi, sel_i]
            v_sel = vb[bi, hi, sel_i]
            s_sel = jnp.einsum('bhqd,bhqnkd->bhqnk', q_i, k_sel).astype(jnp.float32) * scale
            s_sel = jnp.where(valid_i[..., None], s_sel, -jnp.inf).reshape(B_, H, QB, n_sel * L)
            p = jax.nn.softmax(jnp.concatenate([s_sel, s_own], axis=-1), axis=-1).astype(v.dtype)
            p_sel = p[..., :n_sel * L].reshape(B_, H, QB, n_sel, L)
            p_own = p[..., n_sel * L:]
            return (jnp.einsum('bhqnk,bhqnkd->bhqd', p_sel, v_sel)
                    + jnp.einsum('bhqk,bhkd->bhqd', p_own, v_own))
        p_own = jax.nn.softmax(s_own, axis=-1).astype(v.dtype)
        return jnp.einsum('bhqk,bhkd->bhqd', p_own, v_own)

    out = lax.map(attend, (q_steps, sel_steps, valid_steps, t0s))
    out = out.transpose(1, 0, 3, 2, 4).reshape(B_, S, H * dh)
    return out @ w_out


def setup_inputs(seed: int = 0) -> dict:
    key = jax.random.key(seed)
    ks = jax.random.split(key, 16)
    D = D_MODEL
    n_ret, n_sg, n_mo = _n_layers_of(0), _n_layers_of(1), _n_layers_of(2)
    nrm = lambda k, shape, fan_in: jax.random.normal(k, shape, jnp.float32) * (fan_in ** -0.5)
    gain = lambda k, shape: 1.0 + 0.05 * jax.random.normal(k, shape, jnp.float32)
    x = jax.random.normal(ks[0], (BATCH, SEQ, D), jnp.float32)
    positions = jnp.broadcast_to(jnp.arange(SEQ, dtype=jnp.int32), (BATCH, SEQ))
    return {
        "x": x,
        "positions": positions,
        "norm_g": gain(ks[1], (DEPTH, 6, D)),
        "ffn_w_in": nrm(ks[2], (DEPTH, 2, D, 2 * D_FF), D),
        "ffn_w_out": nrm(ks[3], (DEPTH, 2, D_FF, D), D_FF),
        "ret_w_in": nrm(ks[4], (n_ret, D, 4 * D), D),
        "ret_gn_g": gain(ks[5], (n_ret, D)),
        "ret_w_out": nrm(ks[6], (n_ret, D, D), D),
        "sg_w_in": nrm(ks[7], (n_sg, D, 2 * D), D),
        "sg_ln_g": gain(ks[8], (n_sg, D)),
        "sg_w_s": nrm(ks[9], (n_sg, SG_GROUPS, SG_CHUNK, SG_CHUNK), SG_CHUNK),
        "sg_b": 1.0 + 0.1 * jax.random.normal(ks[10], (n_sg, SG_GROUPS, SG_CHUNK), jnp.float32),
        "sg_w_out": nrm(ks[11], (n_sg, D, D), D),
        "moba_w_in": nrm(ks[12], (n_mo, D, 3 * D), D),
        "moba_w_out": nrm(ks[13], (n_mo, D, D), D),
    }


def reference(x, positions, norm_g, ffn_w_in, ffn_w_out, ret_w_in, ret_gn_g, ret_w_out,
              sg_w_in, sg_ln_g, sg_w_s, sg_b, sg_w_out, moba_w_in, moba_w_out):
    h = x
    for i in range(DEPTH):
        m, j = i % N_MIXERS, i // N_MIXERS
        g = norm_g[i]
        h = h + 0.5 * rms_norm(swiglu(rms_norm(h, g[0]), ffn_w_in[i, 0], ffn_w_out[i, 0]), g[1])
        hn = rms_norm(h, g[2])
        if m == 0:
            y = retention_mixer(hn, positions, ret_w_in[j], ret_gn_g[j], ret_w_out[j])
        elif m == 1:
            y = spatial_gating_mixer(hn, sg_w_in[j], sg_ln_g[j], sg_w_s[j], sg_b[j], sg_w_out[j])
        else:
            y = moba_mixer(hn, positions, moba_w_in[j], moba_w_out[j])
        h = h + rms_norm(y, g[3])
        h = h + 0.5 * rms_norm(swiglu(rms_norm(h, g[4]), ffn_w_in[i, 1], ffn_w_out[i, 1]), g[5])
    return h
```

```python
import functools
import math

import jax
import jax.numpy as jnp
from jax import lax
from jax.experimental import pallas as pl
from jax.experimental.pallas import tpu as pltpu

F32 = jnp.float32
BF16 = jnp.bfloat16

D_MODEL = 4096
D_FF = 5632
DEPTH = 4
N_MIXERS = 3
RET_HEADS = 16
RET_DK = D_MODEL // RET_HEADS
RET_CHUNK = 128
SG_GROUPS = 16
SG_CHUNK = 128
SG_DG = D_MODEL // SG_GROUPS
MOBA_HEADS = 32
MOBA_DH = D_MODEL // MOBA_HEADS
MOBA_BLOCK = 256
MOBA_TOPK = 3
ROPE_THETA = 10000.0
EPS = 1e-6

LANE = 128
VMEM_LIMIT = 56 * 1024 * 1024
ROW_TILE = 256
MM_TM = 1024


def _params(n_grid):
    return pltpu.CompilerParams(
        dimension_semantics=("arbitrary",) * n_grid, vmem_limit_bytes=VMEM_LIMIT)


def _rms(x, g):
    return x * lax.rsqrt(jnp.mean(x * x, axis=-1, keepdims=True) + EPS) * g


def _pre_norm_kernel(h_ref, g_ref, xn_ref):
    xn_ref[...] = _rms(h_ref[...], g_ref[...]).astype(BF16)


def pre_norm(h, g):
    m, d = h.shape
    return pl.pallas_call(
        _pre_norm_kernel,
        out_shape=jax.ShapeDtypeStruct((m, d), BF16),
        grid=(m // ROW_TILE,),
        in_specs=[pl.BlockSpec((ROW_TILE, d), lambda i: (i, 0)),
                  pl.BlockSpec((1, d), lambda i: (0, 0))],
        out_specs=pl.BlockSpec((ROW_TILE, d), lambda i: (i, 0)),
        compiler_params=_params(1),
        name="pre_norm",
    )(h, g.reshape(1, d))


def _post_norm_kernel(y_ref, h_ref, gpost_ref, gpre_ref, hout_ref, xn_ref, *, coef):
    h = h_ref[...] + coef * _rms(y_ref[...].astype(F32), gpost_ref[...])
    hout_ref[...] = h
    xn_ref[...] = _rms(h, gpre_ref[...]).astype(BF16)


def _post_norm_last_kernel(y_ref, h_ref, gpost_ref, hout_ref, *, coef):
    hout_ref[...] = h_ref[...] + coef * _rms(y_ref[...].astype(F32), gpost_ref[...])


def post_norm(y, h, g_post, g_pre, coef):
    m, d = h.shape
    row = pl.BlockSpec((ROW_TILE, d), lambda i: (i, 0))
    vec = pl.BlockSpec((1, d), lambda i: (0, 0))
    if g_pre is None:
        return pl.pallas_call(
            functools.partial(_post_norm_last_kernel, coef=coef),
            out_shape=jax.ShapeDtypeStruct((m, d), F32),
            grid=(m // ROW_TILE,),
            in_specs=[row, row, vec],
            out_specs=row,
            compiler_params=_params(1),
            name="post_norm_last",
        )(y, h, g_post.reshape(1, d)), None
    return pl.pallas_call(
        functools.partial(_post_norm_kernel, coef=coef),
        out_shape=(jax.ShapeDtypeStruct((m, d), F32), jax.ShapeDtypeStruct((m, d), BF16)),
        grid=(m // ROW_TILE,),
        in_specs=[row, row, vec, vec],
        out_specs=(row, row),
        compiler_params=_params(1),
        name="post_norm",
    )(y, h, g_post.reshape(1, d), g_pre.reshape(1, d))


def _mm_kernel(*refs, n_parts, n_extra, epilogue):
    x_ref = refs[0]
    w_refs = refs[1:1 + n_parts]
    extra_refs = refs[1 + n_parts:1 + n_parts + n_extra]
    out_ref = refs[1 + n_parts + n_extra]
    x = x_ref[...]
    accs = [jnp.dot(x, w_ref[...], preferred_element_type=F32) for w_ref in w_refs]
    out_ref[...] = epilogue(accs, extra_refs).astype(out_ref.dtype)


def matmul(x, w, *, col_offsets, n_cols, tn, epilogue, out_dtype, row_extras=(), name):
    m, k = x.shape
    tm = MM_TM
    assert m % tm == 0 and n_cols % tn == 0 and all(o % tn == 0 for o in col_offsets)
    in_specs = [pl.BlockSpec((tm, k), lambda i, j: (i, 0))]
    for off in col_offsets:
        in_specs.append(pl.BlockSpec((k, tn), lambda i, j, ob=off // tn: (0, ob + j)))
    for e in row_extras:
        in_specs.append(pl.BlockSpec((tm, e.shape[1]), lambda i, j: (i, 0)))
    kern = functools.partial(_mm_kernel, n_parts=len(col_offsets), n_extra=len(row_extras),
                             epilogue=epilogue)
    return pl.pallas_call(
        kern,
        out_shape=jax.ShapeDtypeStruct((m, n_cols), out_dtype),
        grid=(m // tm, n_cols // tn),
        in_specs=in_specs,
        out_specs=pl.BlockSpec((tm, tn), lambda i, j: (i, j)),
        compiler_params=_params(2),
        name=name,
    )(x, *([w] * len(col_offsets)), *row_extras)


def _ep_plain(accs, extras):
    return accs[0]


def _ep_swiglu(accs, extras):
    a, b = accs
    return a * jax.nn.sigmoid(a) * b


def _ep_silu(accs, extras):
    a = accs[0]
    return a * jax.nn.sigmoid(a)


def _ep_gelu(accs, extras):
    a = accs[0]
    return 0.5 * a * (1.0 + lax.erf(a * math.sqrt(0.5)))


def _ep_rotary_split(accs, extras, *, head_dim, scale):
    acc = accs[0]
    cos = extras[0][...]
    sin = extras[1][...]
    half = head_dim // 2
    outs = []
    for h0 in range(0, acc.shape[1], head_dim):
        t1 = acc[:, h0:h0 + half]
        t2 = acc[:, h0 + half:h0 + head_dim]
        outs.append(t1 * cos - t2 * sin)
        outs.append(t1 * sin + t2 * cos)
    out = jnp.concatenate(outs, axis=1)
    return out * scale if scale != 1.0 else out


def _ep_rotary_roll(accs, extras, *, head_dim):
    acc = accs[0]
    cosf = extras[0][...]
    sins = extras[1][...]
    outs = []
    for h0 in range(0, acc.shape[1], head_dim):
        t = acc[:, h0:h0 + head_dim]
        outs.append(t * cosf + pltpu.roll(t, head_dim // 2, axis=1) * sins)
    return jnp.concatenate(outs, axis=1)


def ffn(xn, w_in, w_out):
    hid = matmul(xn, w_in, col_offsets=(0, D_FF), n_cols=D_FF, tn=512,
                 epilogue=_ep_swiglu, out_dtype=BF16, name="ffn_in")
    return matmul(hid, w_out, col_offsets=(0,), n_cols=D_MODEL, tn=512,
                  epilogue=_ep_plain, out_dtype=F32, name="ffn_out")


def _retention_kernel(lg_ref, q_ref, k_ref, v_ref, sg_ref, gn_ref, o_ref, state_ref, *, seq):
    c = RET_CHUNK
    dk = RET_DK
    lg = lg_ref[0]
    row = lax.broadcasted_iota(jnp.int32, (c, c), 0)
    col = lax.broadcasted_iota(jnp.int32, (c, c), 1)
    dist = (row - col).astype(F32)
    intra_decay = jnp.where(dist >= 0, jnp.exp(lg[:, :c] * jnp.maximum(dist, 0.0)), 0.0)
    idx = lax.broadcasted_iota(jnp.int32, (c, dk), 0).astype(F32)
    q_decay = jnp.exp(lg * (idx + 1.0))
    k_decay = jnp.exp(lg * (c - 1.0 - idx))
    chunk_decay = jnp.exp(lg * float(c))
    gn = gn_ref[...]
    state_ref[...] = jnp.zeros_like(state_ref)

    def chunk(n, carry):
        rows = pl.ds(pl.multiple_of(n * c, c), c)
        qc = q_ref[rows, :]
        kc = k_ref[rows, :]
        vc = v_ref[rows, :]
        scores = lax.dot_general(qc, kc, (((1,), (1,)), ((), ())),
                                 preferred_element_type=F32) * intra_decay
        intra = jnp.dot(scores.astype(BF16), vc, preferred_element_type=F32)
        state = state_ref[...]
        qd = (qc.astype(F32) * q_decay).astype(BF16)
        cross = jnp.dot(qd, state.astype(BF16), preferred_element_type=F32)
        kd = (kc.astype(F32) * k_decay).astype(BF16)
        state_ref[...] = state * chunk_decay + lax.dot_general(
            kd, vc, (((0,), (0,)), ((), ())), preferred_element_type=F32)
        o = intra + cross
        mu = jnp.mean(o, axis=-1, keepdims=True)
        var = jnp.mean(jnp.square(o - mu), axis=-1, keepdims=True)
        on = (o - mu) * lax.rsqrt(var + EPS) * gn
        o_ref[rows, :] = (sg_ref[rows, :].astype(F32) * on).astype(o_ref.dtype)
        return carry

    lax.fori_loop(0, seq // c, chunk, 0)


def retention_core(q, k, v, sg, gn_g, batch, seq):
    m, d = q.shape
    log_gamma = jnp.log1p(-jnp.exp2(-5.0 - jnp.arange(RET_HEADS, dtype=F32)))
    lg = jnp.broadcast_to(log_gamma[:, None, None], (RET_HEADS, 1, RET_DK))
    blk = pl.BlockSpec((seq, RET_DK), lambda b, h: (b, h))
    return pl.pallas_call(
        functools.partial(_retention_kernel, seq=seq),
        out_shape=jax.ShapeDtypeStruct((m, d), BF16),
        grid=(batch, RET_HEADS),
        in_specs=[pl.BlockSpec((1, 1, RET_DK), lambda b, h: (h, 0, 0)),
                  blk, blk, blk, blk,
                  pl.BlockSpec((1, RET_DK), lambda b, h: (0, h))],
        out_specs=blk,
        scratch_shapes=[pltpu.VMEM((RET_DK, RET_DK), F32)],
        compiler_params=_params(2),
        name="retention_core",
    )(lg, q, k, v, sg, gn_g.reshape(1, d))


def retention_mixer(xn, cos, sin, w_in, gn_g, w_out, batch, seq):
    d = D_MODEL
    rot = lambda scale: functools.partial(_ep_rotary_split, head_dim=RET_DK, scale=scale)
    q = matmul(xn, w_in, col_offsets=(0,), n_cols=d, tn=1024, epilogue=rot(1.0),
               out_dtype=BF16, row_extras=(cos, sin), name="ret_q")
    k = matmul(xn, w_in, col_offsets=(d,), n_cols=d, tn=1024, epilogue=rot(RET_DK ** -0.5),
               out_dtype=BF16, row_extras=(cos, sin), name="ret_k")
    v = matmul(xn, w_in, col_offsets=(2 * d,), n_cols=d, tn=1024, epilogue=_ep_plain,
               out_dtype=BF16, name="ret_v")
    sg = matmul(xn, w_in, col_offsets=(3 * d,), n_cols=d, tn=1024, epilogue=_ep_silu,
                out_dtype=BF16, name="ret_g")
    o = retention_core(q, k, v, sg, gn_g, batch, seq)
    return matmul(o, w_out, col_offsets=(0,), n_cols=d, tn=1024, epilogue=_ep_plain,
                  out_dtype=F32, name="ret_out")


def _sgu_kernel(u_ref, v_ref, lng_ref, ws_ref, bs_ref, o_ref):
    c = SG_CHUNK
    v = v_ref[...].astype(F32)
    mu = jnp.mean(v, axis=-1, keepdims=True)
    var = jnp.mean(jnp.square(v - mu), axis=-1, keepdims=True)
    vn = ((v - mu) * lax.rsqrt(var + EPS) * lng_ref[...]).astype(BF16)
    row = lax.broadcasted_iota(jnp.int32, (c, c), 0)
    col = lax.broadcasted_iota(jnp.int32, (c, c), 1)
    causal = row >= col
    bs = bs_ref[...]
    for g in range(SG_GROUPS):
        cols = slice(g * SG_DG, (g + 1) * SG_DG)
        w_m = jnp.where(causal, ws_ref[g], 0.0).astype(BF16)
        mixed = jnp.dot(w_m, vn[:, cols], preferred_element_type=F32) + bs[:, g:g + 1]
        o_ref[:, cols] = (u_ref[:, cols].astype(F32) * mixed).astype(o_ref.dtype)


def sgu_core(uv, ln_g, w_s, b_s):
    m = uv.shape[0]
    d = D_MODEL
    c = SG_CHUNK
    return pl.pallas_call(
        _sgu_kernel,
        out_shape=jax.ShapeDtypeStruct((m, d), BF16),
        grid=(m // c,),
        in_specs=[pl.BlockSpec((c, d), lambda n: (n, 0)),
                  pl.BlockSpec((c, d), lambda n: (n, 1)),
                  pl.BlockSpec((1, d), lambda n: (0, 0)),
                  pl.BlockSpec((SG_GROUPS, c, c), lambda n: (0, 0, 0)),
                  pl.BlockSpec((c, SG_GROUPS), lambda n: (0, 0))],
        out_specs=pl.BlockSpec((c, d), lambda n: (n, 0)),
        compiler_params=_params(1),
        name="sgu_core",
    )(uv, uv, ln_g.reshape(1, d), w_s, b_s.T)


def spatial_gating_mixer(xn, w_in, ln_g, w_s, b_s, w_out):
    d = D_MODEL
    uv = matmul(xn, w_in, col_offsets=(0,), n_cols=2 * d, tn=1024, epilogue=_ep_gelu,
                out_dtype=BF16, name="sg_in")
    o = sgu_core(uv, ln_g, w_s, b_s)
    return matmul(o, w_out, col_offsets=(0,), n_cols=d, tn=1024, epilogue=_ep_plain,
                  out_dtype=F32, name="sg_out")


_NT = (((1,), (1,)), ((), ()))


def _moba_kernel(q_ref, k_ref, v_ref, o_ref, *, seq):
    blk = MOBA_BLOCK
    nblk = seq // blk
    n_sel = min(MOBA_TOPK, nblk - 1)
    scale = MOBA_DH ** -0.5
    neg_inf = float("-inf")

    k_mean = jnp.concatenate(
        [jnp.mean(k_ref[j * blk:(j + 1) * blk, :].astype(F32), axis=0, keepdims=True)
         for j in range(nblk)], axis=0)
    k_mean_hi = k_mean.astype(BF16)
    k_mean_lo = (k_mean - k_mean_hi.astype(F32)).astype(BF16)

    row = lax.broadcasted_iota(jnp.int32, (blk, blk), 0)
    col = lax.broadcasted_iota(jnp.int32, (blk, blk), 1)
    causal = col <= row

    for i in range(nblk):
        q = q_ref[i * blk:(i + 1) * blk, :]
        nk = (i + 1) * blk
        s = lax.dot_general(q, k_ref[0:nk, :], _NT, preferred_element_type=F32) * scale
        if i > n_sel:
            gate = (lax.dot_general(q, k_mean_hi, _NT, preferred_element_type=F32)
                    + lax.dot_general(q, k_mean_lo, _NT, preferred_element_type=F32))
            g = [gate[:, j:j + 1] for j in range(i)]
        parts = []
        for j in range(i):
            sj = s[:, j * blk:(j + 1) * blk]
            if i > n_sel:
                rank = jnp.zeros((blk, 1), jnp.int32)
                for jj in range(i):
                    if jj < j:
                        rank += (g[jj] >= g[j]).astype(jnp.int32)
                    elif jj > j:
                        rank += (g[jj] > g[j]).astype(jnp.int32)
                sj = jnp.where(rank < n_sel, sj, neg_inf)
            parts.append(sj)
        parts.append(jnp.where(causal, s[:, i * blk:nk], neg_inf))
        m = parts[0].max(axis=-1, keepdims=True)
        for part in parts[1:]:
            m = jnp.maximum(m, part.max(axis=-1, keepdims=True))
        p = [jnp.exp(part - m) for part in parts]
        l = p[0].sum(axis=-1, keepdims=True)
        for part in p[1:]:
            l += part.sum(axis=-1, keepdims=True)
        pb = jnp.concatenate([part.astype(BF16) for part in p], axis=1)
        o = jnp.dot(pb, v_ref[0:nk, :], preferred_element_type=F32)
        o_ref[i * blk:(i + 1) * blk, :] = (o / l).astype(o_ref.dtype)


def moba_core(q, k, v, batch, seq):
    m, d = q.shape
    assert seq % MOBA_BLOCK == 0
    blk = pl.BlockSpec((seq, MOBA_DH), lambda b, h: (b, h))
    return pl.pallas_call(
        functools.partial(_moba_kernel, seq=seq),
        out_shape=jax.ShapeDtypeStruct((m, d), BF16),
        grid=(batch, MOBA_HEADS),
        in_specs=[blk, blk, blk],
        out_specs=blk,
        compiler_params=_params(2),
        name="moba_core",
    )(q, k, v)


def moba_mixer(xn, cosf, sins, w_in, w_out, batch, seq):
    d = D_MODEL
    rot = functools.partial(_ep_rotary_roll, head_dim=MOBA_DH)
    q = matmul(xn, w_in, col_offsets=(0,), n_cols=d, tn=1024, epilogue=rot,
               out_dtype=BF16, row_extras=(cosf, sins), name="moba_q")
    k = matmul(xn, w_in, col_offsets=(d,), n_cols=d, tn=1024, epilogue=rot,
               out_dtype=BF16, row_extras=(cosf, sins), name="moba_k")
    v = matmul(xn, w_in, col_offsets=(2 * d,), n_cols=d, tn=1024, epilogue=_ep_plain,
               out_dtype=BF16, name="moba_v")
    o = moba_core(q, k, v, batch, seq)
    return matmul(o, w_out, col_offsets=(0,), n_cols=d, tn=1024, epilogue=_ep_plain,
                  out_dtype=F32, name="moba_out")


def kernel(x, positions, norm_g, ffn_w_in, ffn_w_out, ret_w_in, ret_gn_g, ret_w_out,
           sg_w_in, sg_ln_g, sg_w_s, sg_b, sg_w_out, moba_w_in, moba_w_out):
    batch, seq, d = x.shape
    m = batch * seq
    pos = positions.reshape(m, 1).astype(F32)

    ret_inv = 1.0 / (10000.0 ** jnp.linspace(0.0, 1.0, RET_DK // 2, dtype=F32))
    ret_ang = pos * ret_inv
    ret_cos, ret_sin = jnp.cos(ret_ang), jnp.sin(ret_ang)
    moba_inv = 1.0 / (ROPE_THETA ** (jnp.arange(0, MOBA_DH, 2, dtype=F32) / MOBA_DH))
    moba_ang = pos * moba_inv
    mc, ms = jnp.cos(moba_ang), jnp.sin(moba_ang)
    moba_cosf = jnp.concatenate([mc, mc], axis=1)
    moba_sins = jnp.concatenate([-ms, ms], axis=1)

    bf = lambda w: w.astype(BF16)
    h = x.reshape(m, d)
    xn = pre_norm(h, norm_g[0, 0])
    for i in range(DEPTH):
        mix, j = i % N_MIXERS, i // N_MIXERS
        g = norm_g[i]
        y = ffn(xn, bf(ffn_w_in[i, 0]), bf(ffn_w_out[i, 0]))
        h, xn = post_norm(y, h, g[1], g[2], 0.5)
        if mix == 0:
            y = retention_mixer(xn, ret_cos, ret_sin, bf(ret_w_in[j]), ret_gn_g[j],
                                bf(ret_w_out[j]), batch, seq)
        elif mix == 1:
            y = spatial_gating_mixer(xn, bf(sg_w_in[j]), sg_ln_g[j], sg_w_s[j], sg_b[j],
                                     bf(sg_w_out[j]))
        else:
            y = moba_mixer(xn, moba_cosf, moba_sins, bf(moba_w_in[j]), bf(moba_w_out[j]),
                           batch, seq)
        h, xn = post_norm(y, h, g[3], g[4], 1.0)
        y = ffn(xn, bf(ffn_w_in[i, 1]), bf(ffn_w_out[i, 1]))
        g_next = norm_g[i + 1, 0] if i + 1 < DEPTH else None
        h, xn = post_norm(y, h, g[5], g_next, 0.5)
    return h.reshape(batch, seq, d)
```

```python
import functools
import math

import jax
import jax.numpy as jnp
from jax import lax
from jax.experimental import pallas as pl
from jax.experimental.pallas import tpu as pltpu

F32 = jnp.float32
BF16 = jnp.bfloat16

D_MODEL = 4096
D_FF = 5632
DEPTH = 4
N_MIXERS = 3
RET_HEADS = 16
RET_DK = D_MODEL // RET_HEADS
RET_CHUNK = 128
RET_UNROLL = 4
SG_GROUPS = 16
SG_CHUNK = 128
SG_DG = D_MODEL // SG_GROUPS
MOBA_HEADS = 32
MOBA_DH = D_MODEL // MOBA_HEADS
MOBA_BLOCK = 256
MOBA_TOPK = 3
ROPE_THETA = 10000.0
EPS = 1e-6

LANE = 128
VMEM_LIMIT = 56 * 1024 * 1024
ROW_TILE = 256
MM_TM = 1024
MM_W_BLOCK_BYTES = 8 * 1024 * 1024


def _params(n_grid):
    return pltpu.CompilerParams(
        dimension_semantics=("arbitrary",) * n_grid, vmem_limit_bytes=VMEM_LIMIT)


def _rms(x, g):
    return x * lax.rsqrt(jnp.mean(x * x, axis=-1, keepdims=True) + EPS) * g


def _pre_norm_kernel(h_ref, g_ref, xn_ref):
    xn_ref[...] = _rms(h_ref[...], g_ref[...]).astype(BF16)


def pre_norm(h, g):
    m, d = h.shape
    return pl.pallas_call(
        _pre_norm_kernel,
        out_shape=jax.ShapeDtypeStruct((m, d), BF16),
        grid=(m // ROW_TILE,),
        in_specs=[pl.BlockSpec((ROW_TILE, d), lambda i: (i, 0)),
                  pl.BlockSpec((1, d), lambda i: (0, 0))],
        out_specs=pl.BlockSpec((ROW_TILE, d), lambda i: (i, 0)),
        compiler_params=_params(1),
        name="pre_norm",
    )(h, g.reshape(1, d))


def _post_norm_kernel(y_ref, h_ref, gpost_ref, gpre_ref, hout_ref, xn_ref, *, coef):
    h = h_ref[...] + coef * _rms(y_ref[...].astype(F32), gpost_ref[...])
    hout_ref[...] = h
    xn_ref[...] = _rms(h, gpre_ref[...]).astype(BF16)


def _post_norm_last_kernel(y_ref, h_ref, gpost_ref, hout_ref, *, coef):
    hout_ref[...] = h_ref[...] + coef * _rms(y_ref[...].astype(F32), gpost_ref[...])


def post_norm(y, h, g_post, g_pre, coef):
    m, d = h.shape
    row = pl.BlockSpec((ROW_TILE, d), lambda i: (i, 0))
    vec = pl.BlockSpec((1, d), lambda i: (0, 0))
    if g_pre is None:
        return pl.pallas_call(
            functools.partial(_post_norm_last_kernel, coef=coef),
            out_shape=jax.ShapeDtypeStruct((m, d), F32),
            grid=(m // ROW_TILE,),
            in_specs=[row, row, vec],
            out_specs=row,
            compiler_params=_params(1),
            name="post_norm_last",
        )(y, h, g_post.reshape(1, d)), None
    return pl.pallas_call(
        functools.partial(_post_norm_kernel, coef=coef),
        out_shape=(jax.ShapeDtypeStruct((m, d), F32), jax.ShapeDtypeStruct((m, d), BF16)),
        grid=(m // ROW_TILE,),
        in_specs=[row, row, vec, vec],
        out_specs=(row, row),
        compiler_params=_params(1),
        name="post_norm",
    )(y, h, g_post.reshape(1, d), g_pre.reshape(1, d))


def _mm_kernel(*refs, n_parts, n_extra, epilogue):
    x_ref = refs[0]
    w_refs = refs[1:1 + n_parts]
    extra_refs = refs[1 + n_parts:1 + n_parts + n_extra]
    out_ref = refs[1 + n_parts + n_extra]
    x = x_ref[...]
    accs = [jnp.dot(x, w_ref[...].astype(BF16), preferred_element_type=F32) for w_ref in w_refs]
    out_ref[...] = epilogue(accs, extra_refs).astype(out_ref.dtype)


def _mm_col_tile(k, n_parts):
    tn = LANE
    while n_parts * k * (2 * tn) * 4 <= MM_W_BLOCK_BYTES:
        tn *= 2
    return tn


def matmul(x, w, lead, *, col_offsets, n_cols, epilogue, out_dtype, row_extras=(), name):
    m, k = x.shape
    tm = MM_TM
    tn = _mm_col_tile(k, len(col_offsets))
    assert m % tm == 0 and n_cols % tn == 0 and all(o % tn == 0 for o in col_offsets)
    assert w.shape[len(lead)] == k
    in_specs = [pl.BlockSpec((tm, k), lambda i, j: (i, 0))]
    w_block = (None,) * len(lead) + (k, tn)
    for off in col_offsets:
        in_specs.append(pl.BlockSpec(w_block, lambda i, j, ob=off // tn: (*lead, 0, ob + j)))
    for e in row_extras:
        in_specs.append(pl.BlockSpec((tm, e.shape[1]), lambda i, j: (i, 0)))
    kern = functools.partial(_mm_kernel, n_parts=len(col_offsets), n_extra=len(row_extras),
                             epilogue=epilogue)
    return pl.pallas_call(
        kern,
        out_shape=jax.ShapeDtypeStruct((m, n_cols), out_dtype),
        grid=(m // tm, n_cols // tn),
        in_specs=in_specs,
        out_specs=pl.BlockSpec((tm, tn), lambda i, j: (i, j)),
        compiler_params=_params(2),
        name=name,
    )(x, *([w] * len(col_offsets)), *row_extras)


def _ep_plain(accs, extras):
    return accs[0]


def _ep_swiglu(accs, extras):
    a, b = accs
    return a * jax.nn.sigmoid(a) * b


def _ep_silu(accs, extras):
    a = accs[0]
    return a * jax.nn.sigmoid(a)


def _ep_gelu(accs, extras):
    a = accs[0]
    return 0.5 * a * (1.0 + lax.erf(a * math.sqrt(0.5)))


def _ep_rotary_split(accs, extras, *, head_dim, scale):
    acc = accs[0]
    cos = extras[0][...]
    sin = extras[1][...]
    half = head_dim // 2
    outs = []
    for h0 in range(0, acc.shape[1], head_dim):
        t1 = acc[:, h0:h0 + half]
        t2 = acc[:, h0 + half:h0 + head_dim]
        outs.append(t1 * cos - t2 * sin)
        outs.append(t1 * sin + t2 * cos)
    out = jnp.concatenate(outs, axis=1)
    return out * scale if scale != 1.0 else out


def _ep_rotary_roll(accs, extras, *, head_dim, scale):
    acc = accs[0]
    cosf = extras[0][...]
    sins = extras[1][...]
    outs = []
    for h0 in range(0, acc.shape[1], head_dim):
        t = acc[:, h0:h0 + head_dim]
        outs.append(t * cosf + pltpu.roll(t, head_dim // 2, axis=1) * sins)
    out = jnp.concatenate(outs, axis=1)
    return out * scale if scale != 1.0 else out


def ffn(xn, w_in, w_out, lead):
    hid = matmul(xn, w_in, lead, col_offsets=(0, D_FF), n_cols=D_FF,
                 epilogue=_ep_swiglu, out_dtype=BF16, name="ffn_in")
    return matmul(hid, w_out, lead, col_offsets=(0,), n_cols=D_MODEL,
                  epilogue=_ep_plain, out_dtype=BF16, name="ffn_out")


_NT = (((1,), (1,)), ((), ()))
_TN = (((0,), (0,)), ((), ()))


def _retention_kernel(lg_ref, q_ref, k_ref, v_ref, sg_ref, gn_ref, o_ref, state_ref, *, seq):
    c = RET_CHUNK
    dk = RET_DK
    lg = lg_ref[0]
    row = lax.broadcasted_iota(jnp.int32, (c, c), 0)
    col = lax.broadcasted_iota(jnp.int32, (c, c), 1)
    dist = (row - col).astype(F32)
    intra_decay = jnp.where(dist >= 0, jnp.exp(lg[:, :c] * jnp.maximum(dist, 0.0)), 0.0)
    idx = lax.broadcasted_iota(jnp.int32, (c, dk), 0).astype(F32)
    q_decay = jnp.exp(lg * (idx + 1.0))
    k_decay = jnp.exp(lg * (c - 1.0 - idx))
    chunk_decay = jnp.exp(lg * float(c))
    gn = gn_ref[...]
    state_ref[...] = jnp.zeros_like(state_ref)

    def chunk(n, carry):
        rows = pl.ds(pl.multiple_of(n * c, c), c)
        qc = q_ref[rows, :]
        kc = k_ref[rows, :]
        vc = v_ref[rows, :]
        scores = lax.dot_general(qc, kc, _NT, preferred_element_type=F32) * intra_decay
        intra = jnp.dot(scores.astype(BF16), vc, preferred_element_type=F32)
        state = state_ref[...]
        qd = (qc.astype(F32) * q_decay).astype(BF16)
        cross = jnp.dot(qd, state.astype(BF16), preferred_element_type=F32)
        kd = (kc.astype(F32) * k_decay).astype(BF16)
        state_ref[...] = state * chunk_decay + lax.dot_general(
            kd, vc, _TN, preferred_element_type=F32)
        o = intra + cross
        mu = jnp.mean(o, axis=-1, keepdims=True)
        var = jnp.mean(jnp.square(o - mu), axis=-1, keepdims=True)
        on = (o - mu) * lax.rsqrt(var + EPS) * gn
        o_ref[rows, :] = (sg_ref[rows, :].astype(F32) * on).astype(o_ref.dtype)
        return carry

    lax.fori_loop(0, seq // c, chunk, 0, unroll=RET_UNROLL)


def retention_core(q, k, v, sg, gn_g, batch, seq):
    m, d = q.shape
    log_gamma = jnp.log1p(-jnp.exp2(-5.0 - jnp.arange(RET_HEADS, dtype=F32)))
    lg = jnp.broadcast_to(log_gamma[:, None, None], (RET_HEADS, 1, RET_DK))
    blk = pl.BlockSpec((seq, RET_DK), lambda b, h: (b, h))
    return pl.pallas_call(
        functools.partial(_retention_kernel, seq=seq),
        out_shape=jax.ShapeDtypeStruct((m, d), BF16),
        grid=(batch, RET_HEADS),
        in_specs=[pl.BlockSpec((1, 1, RET_DK), lambda b, h: (h, 0, 0)),
                  blk, blk, blk, blk,
                  pl.BlockSpec((1, RET_DK), lambda b, h: (0, h))],
        out_specs=blk,
        scratch_shapes=[pltpu.VMEM((RET_DK, RET_DK), F32)],
        compiler_params=_params(2),
        name="retention_core",
    )(lg, q, k, v, sg, gn_g.reshape(1, d))


def retention_mixer(xn, cos, sin, w_in, gn_g, w_out, j, batch, seq):
    d = D_MODEL
    rot = lambda scale: functools.partial(_ep_rotary_split, head_dim=RET_DK, scale=scale)
    q = matmul(xn, w_in, (j,), col_offsets=(0,), n_cols=d, epilogue=rot(1.0),
               out_dtype=BF16, row_extras=(cos, sin), name="ret_q")
    k = matmul(xn, w_in, (j,), col_offsets=(d,), n_cols=d, epilogue=rot(RET_DK ** -0.5),
               out_dtype=BF16, row_extras=(cos, sin), name="ret_k")
    v = matmul(xn, w_in, (j,), col_offsets=(2 * d,), n_cols=d, epilogue=_ep_plain,
               out_dtype=BF16, name="ret_v")
    sg = matmul(xn, w_in, (j,), col_offsets=(3 * d,), n_cols=d, epilogue=_ep_silu,
                out_dtype=BF16, name="ret_g")
    o = retention_core(q, k, v, sg, gn_g[j], batch, seq)
    return matmul(o, w_out, (j,), col_offsets=(0,), n_cols=d, epilogue=_ep_plain,
                  out_dtype=BF16, name="ret_out")


def _sgu_kernel(u_ref, v_ref, lng_ref, ws_ref, bs_ref, o_ref):
    c = SG_CHUNK
    v = v_ref[...].astype(F32)
    mu = jnp.mean(v, axis=-1, keepdims=True)
    var = jnp.mean(jnp.square(v - mu), axis=-1, keepdims=True)
    vn = ((v - mu) * lax.rsqrt(var + EPS) * lng_ref[...]).astype(BF16)
    row = lax.broadcasted_iota(jnp.int32, (c, c), 0)
    col = lax.broadcasted_iota(jnp.int32, (c, c), 1)
    causal = row >= col
    bs = bs_ref[...]
    for g in range(SG_GROUPS):
        cols = slice(g * SG_DG, (g + 1) * SG_DG)
        w_m = jnp.where(causal, ws_ref[g], 0.0).astype(BF16)
        mixed = jnp.dot(w_m, vn[:, cols], preferred_element_type=F32) + bs[:, g:g + 1]
        o_ref[:, cols] = (u_ref[:, cols].astype(F32) * mixed).astype(o_ref.dtype)


def sgu_core(uv, ln_g, w_s, b_s):
    m = uv.shape[0]
    d = D_MODEL
    c = SG_CHUNK
    return pl.pallas_call(
        _sgu_kernel,
        out_shape=jax.ShapeDtypeStruct((m, d), BF16),
        grid=(m // c,),
        in_specs=[pl.BlockSpec((c, d), lambda n: (n, 0)),
                  pl.BlockSpec((c, d), lambda n: (n, 1)),
                  pl.BlockSpec((1, d), lambda n: (0, 0)),
                  pl.BlockSpec((SG_GROUPS, c, c), lambda n: (0, 0, 0)),
                  pl.BlockSpec((c, SG_GROUPS), lambda n: (0, 0))],
        out_specs=pl.BlockSpec((c, d), lambda n: (n, 0)),
        compiler_params=_params(1),
        name="sgu_core",
    )(uv, uv, ln_g.reshape(1, d), w_s, b_s.T)


def spatial_gating_mixer(xn, w_in, ln_g, w_s, b_s, w_out, j):
    d = D_MODEL
    uv = matmul(xn, w_in, (j,), col_offsets=(0,), n_cols=2 * d, epilogue=_ep_gelu,
                out_dtype=BF16, name="sg_in")
    o = sgu_core(uv, ln_g[j], w_s[j], b_s[j])
    return matmul(o, w_out, (j,), col_offsets=(0,), n_cols=d, epilogue=_ep_plain,
                  out_dtype=BF16, name="sg_out")


def _moba_kernel(q_ref, k_ref, v_ref, o_ref, vaug_ref, *, seq):
    blk = MOBA_BLOCK
    dh = MOBA_DH
    nblk = seq // blk
    n_sel = min(MOBA_TOPK, nblk - 1)
    neg_inf = float("-inf")

    vaug_ref[:, :dh] = v_ref[...]
    vaug_ref[:, dh:] = jnp.ones((seq, dh), vaug_ref.dtype)

    k_mean = jnp.concatenate(
        [jnp.broadcast_to(
            jnp.mean(k_ref[j * blk:(j + 1) * blk, :].astype(F32), axis=0, keepdims=True),
            (LANE, dh)) for j in range(nblk - 1)], axis=0)
    k_mean_hi = k_mean.astype(BF16)
    k_mean_lo = (k_mean - k_mean_hi.astype(F32)).astype(BF16)

    row = lax.broadcasted_iota(jnp.int32, (blk, LANE), 0)
    lane = lax.broadcasted_iota(jnp.int32, (blk, LANE), 1)

    for i in range(nblk):
        q = q_ref[i * blk:(i + 1) * blk, :]
        nk = (i + 1) * blk
        s = lax.dot_general(q, k_ref[0:nk, :], _NT, preferred_element_type=F32)
        chunks = [s[:, c * LANE:(c + 1) * LANE] for c in range(nk // LANE)]
        if i > n_sel:
            gate = (lax.dot_general(q, k_mean_hi[:i * LANE], _NT, preferred_element_type=F32)
                    + lax.dot_general(q, k_mean_lo[:i * LANE], _NT, preferred_element_type=F32))
            g = [gate[:, j * LANE:(j + 1) * LANE] for j in range(i)]
            rank = [jnp.zeros((blk, LANE), F32) for _ in range(i)]
            for a in range(i):
                for b in range(a + 1, i):
                    a_first = g[a] >= g[b]
                    rank[b] = rank[b] + jnp.where(a_first, 1.0, 0.0)
                    rank[a] = rank[a] + jnp.where(a_first, 0.0, 1.0)
            for j in range(i):
                sel = rank[j] < float(n_sel)
                for c in range(j * blk // LANE, (j + 1) * blk // LANE):
                    chunks[c] = jnp.where(sel, chunks[c], neg_inf)
        for t, c in enumerate(range(i * blk // LANE, nk // LANE)):
            chunks[c] = jnp.where(lane + t * LANE <= row, chunks[c], neg_inf)
        m_el = chunks[0]
        for ch in chunks[1:]:
            m_el = jnp.maximum(m_el, ch)
        m = m_el.max(axis=-1, keepdims=True)
        p = jnp.concatenate([jnp.exp(ch - m).astype(BF16) for ch in chunks], axis=1)
        o_aug = jnp.dot(p, vaug_ref[0:nk, :], preferred_element_type=F32)
        o_ref[i * blk:(i + 1) * blk, :] = (o_aug[:, :dh] / o_aug[:, dh:]).astype(o_ref.dtype)


def moba_core(q, k, v, batch, seq):
    m, d = q.shape
    assert seq % MOBA_BLOCK == 0 and MOBA_DH == LANE
    blk = pl.BlockSpec((seq, MOBA_DH), lambda b, h: (b, h))
    return pl.pallas_call(
        functools.partial(_moba_kernel, seq=seq),
        out_shape=jax.ShapeDtypeStruct((m, d), BF16),
        grid=(batch, d // MOBA_DH),
        in_specs=[blk, blk, blk],
        out_specs=blk,
        scratch_shapes=[pltpu.VMEM((seq, 2 * MOBA_DH), BF16)],
        compiler_params=_params(2),
        name="moba_core",
    )(q, k, v)


def moba_mixer(xn, cosf, sins, w_in, w_out, j, batch, seq):
    d = D_MODEL
    rot = lambda scale: functools.partial(_ep_rotary_roll, head_dim=MOBA_DH, scale=scale)
    q = matmul(xn, w_in, (j,), col_offsets=(0,), n_cols=d, epilogue=rot(MOBA_DH ** -0.5),
               out_dtype=BF16, row_extras=(cosf, sins), name="moba_q")
    k = matmul(xn, w_in, (j,), col_offsets=(d,), n_cols=d, epilogue=rot(1.0),
               out_dtype=BF16, row_extras=(cosf, sins), name="moba_k")
    v = matmul(xn, w_in, (j,), col_offsets=(2 * d,), n_cols=d, epilogue=_ep_plain,
               out_dtype=BF16, name="moba_v")
    o = moba_core(q, k, v, batch, seq)
    return matmul(o, w_out, (j,), col_offsets=(0,), n_cols=d, epilogue=_ep_plain,
                  out_dtype=BF16, name="moba_out")


def kernel(x, positions, norm_g, ffn_w_in, ffn_w_out, ret_w_in, ret_gn_g, ret_w_out,
           sg_w_in, sg_ln_g, sg_w_s, sg_b, sg_w_out, moba_w_in, moba_w_out):
    batch, seq, d = x.shape
    m = batch * seq
    pos = positions.reshape(m, 1).astype(F32)

    ret_inv = 1.0 / (10000.0 ** jnp.linspace(0.0, 1.0, RET_DK // 2, dtype=F32))
    ret_ang = pos * ret_inv
    ret_cos, ret_sin = jnp.cos(ret_ang), jnp.sin(ret_ang)
    moba_inv = 1.0 / (ROPE_THETA ** (jnp.arange(0, MOBA_DH, 2, dtype=F32) / MOBA_DH))
    moba_ang = pos * moba_inv
    mc, ms = jnp.cos(moba_ang), jnp.sin(moba_ang)
    moba_cosf = jnp.concatenate([mc, mc], axis=1)
    moba_sins = jnp.concatenate([-ms, ms], axis=1)

    h = x.reshape(m, d)
    xn = pre_norm(h, norm_g[0, 0])
    for i in range(DEPTH):
        mix, j = i % N_MIXERS, i // N_MIXERS
        g = norm_g[i]
        y = ffn(xn, ffn_w_in, ffn_w_out, (i, 0))
        h, xn = post_norm(y, h, g[1], g[2], 0.5)
        if mix == 0:
            y = retention_mixer(xn, ret_cos, ret_sin, ret_w_in, ret_gn_g, ret_w_out,
                                j, batch, seq)
        elif mix == 1:
            y = spatial_gating_mixer(xn, sg_w_in, sg_ln_g, sg_w_s, sg_b, sg_w_out, j)
        else:
            y = moba_mixer(xn, moba_cosf, moba_sins, moba_w_in, moba_w_out, j, batch, seq)
        h, xn = post_norm(y, h, g[3], g[4], 1.0)
        y = ffn(xn, ffn_w_in, ffn_w_out, (i, 1))
        g_next = norm_g[i + 1, 0] if i + 1 < DEPTH else None
        h, xn = post_norm(y, h, g[5], g_next, 0.5)
    return h.reshape(batch, seq, d)
```

```python
import functools
import math

import jax
import jax.numpy as jnp
from jax import lax
from jax.experimental import pallas as pl
from jax.experimental.pallas import tpu as pltpu

F32 = jnp.float32
BF16 = jnp.bfloat16

D_MODEL = 4096
D_FF = 5632
DEPTH = 4
N_MIXERS = 3
RET_HEADS = 16
RET_DK = D_MODEL // RET_HEADS
RET_CHUNK = 128
RET_UNROLL = 8
SG_GROUPS = 16
SG_CHUNK = 128
SG_DG = D_MODEL // SG_GROUPS
MOBA_HEADS = 32
MOBA_DH = D_MODEL // MOBA_HEADS
MOBA_BLOCK = 256
MOBA_TOPK = 3
ROPE_THETA = 10000.0
EPS = 1e-6

LANE = 128
VMEM_LIMIT = 56 * 1024 * 1024
ROW_TILE = 256
MM_TM = 1024
MM_W_BLOCK_BYTES = 8 * 1024 * 1024


def _params(n_grid):
    return pltpu.CompilerParams(
        dimension_semantics=("arbitrary",) * n_grid, vmem_limit_bytes=VMEM_LIMIT)


def _rms(x, g):
    return x * lax.rsqrt(jnp.mean(x * x, axis=-1, keepdims=True) + EPS) * g


def _pre_norm_kernel(h_ref, g_ref, xn_ref):
    xn_ref[...] = _rms(h_ref[...], g_ref[...]).astype(BF16)


def pre_norm(h, g):
    m, d = h.shape
    return pl.pallas_call(
        _pre_norm_kernel,
        out_shape=jax.ShapeDtypeStruct((m, d), BF16),
        grid=(m // ROW_TILE,),
        in_specs=[pl.BlockSpec((ROW_TILE, d), lambda i: (i, 0)),
                  pl.BlockSpec((1, d), lambda i: (0, 0))],
        out_specs=pl.BlockSpec((ROW_TILE, d), lambda i: (i, 0)),
        compiler_params=_params(1),
        name="pre_norm",
    )(h, g.reshape(1, d))


def _post_norm_kernel(y_ref, h_ref, gpost_ref, gpre_ref, hout_ref, xn_ref, *, coef):
    h = h_ref[...] + coef * _rms(y_ref[...].astype(F32), gpost_ref[...])
    hout_ref[...] = h
    xn_ref[...] = _rms(h, gpre_ref[...]).astype(BF16)


def _post_norm_last_kernel(y_ref, h_ref, gpost_ref, hout_ref, *, coef):
    hout_ref[...] = h_ref[...] + coef * _rms(y_ref[...].astype(F32), gpost_ref[...])


def post_norm(y, h, g_post, g_pre, coef):
    m, d = h.shape
    row = pl.BlockSpec((ROW_TILE, d), lambda i: (i, 0))
    vec = pl.BlockSpec((1, d), lambda i: (0, 0))
    if g_pre is None:
        return pl.pallas_call(
            functools.partial(_post_norm_last_kernel, coef=coef),
            out_shape=jax.ShapeDtypeStruct((m, d), F32),
            grid=(m // ROW_TILE,),
            in_specs=[row, row, vec],
            out_specs=row,
            compiler_params=_params(1),
            name="post_norm_last",
        )(y, h, g_post.reshape(1, d)), None
    return pl.pallas_call(
        functools.partial(_post_norm_kernel, coef=coef),
        out_shape=(jax.ShapeDtypeStruct((m, d), F32), jax.ShapeDtypeStruct((m, d), BF16)),
        grid=(m // ROW_TILE,),
        in_specs=[row, row, vec, vec],
        out_specs=(row, row),
        compiler_params=_params(1),
        name="post_norm",
    )(y, h, g_post.reshape(1, d), g_pre.reshape(1, d))


def _mm_kernel(*refs, n_parts, n_extra, epilogue):
    x_ref = refs[0]
    w_refs = refs[1:1 + n_parts]
    extra_refs = refs[1 + n_parts:1 + n_parts + n_extra]
    out_ref = refs[1 + n_parts + n_extra]
    x = x_ref[...]
    accs = [jnp.dot(x, w_ref[...].astype(BF16), preferred_element_type=F32) for w_ref in w_refs]
    out_ref[...] = epilogue(accs, extra_refs).astype(out_ref.dtype)


def _mm_col_tile(k, n_parts):
    tn = LANE
    while n_parts * k * (2 * tn) * 4 <= MM_W_BLOCK_BYTES:
        tn *= 2
    return tn


def matmul(x, w, lead, *, col_offsets, n_cols, epilogue, out_dtype, row_extras=(), name):
    m, k = x.shape
    tm = MM_TM
    tn = _mm_col_tile(k, len(col_offsets))
    assert m % tm == 0 and n_cols % tn == 0 and all(o % tn == 0 for o in col_offsets)
    assert w.shape[len(lead)] == k
    in_specs = [pl.BlockSpec((tm, k), lambda i, j: (i, 0))]
    w_block = (None,) * len(lead) + (k, tn)
    for off in col_offsets:
        in_specs.append(pl.BlockSpec(w_block, lambda i, j, ob=off // tn: (*lead, 0, ob + j)))
    for e in row_extras:
        in_specs.append(pl.BlockSpec((tm, e.shape[1]), lambda i, j: (i, 0)))
    kern = functools.partial(_mm_kernel, n_parts=len(col_offsets), n_extra=len(row_extras),
                             epilogue=epilogue)
    return pl.pallas_call(
        kern,
        out_shape=jax.ShapeDtypeStruct((m, n_cols), out_dtype),
        grid=(m // tm, n_cols // tn),
        in_specs=in_specs,
        out_specs=pl.BlockSpec((tm, tn), lambda i, j: (i, j)),
        compiler_params=_params(2),
        name=name,
    )(x, *([w] * len(col_offsets)), *row_extras)


def _ep_plain(accs, extras):
    return accs[0]


def _ep_swiglu(accs, extras):
    a, b = accs
    return a * jax.nn.sigmoid(a) * b


def _ep_silu(accs, extras):
    a = accs[0]
    return a * jax.nn.sigmoid(a)


def _ep_gelu(accs, extras):
    a = accs[0]
    return 0.5 * a * (1.0 + lax.erf(a * math.sqrt(0.5)))


def _ep_rotary_split(accs, extras, *, head_dim, scale):
    acc = accs[0]
    cos = extras[0][...]
    sin = extras[1][...]
    half = head_dim // 2
    outs = []
    for h0 in range(0, acc.shape[1], head_dim):
        t1 = acc[:, h0:h0 + half]
        t2 = acc[:, h0 + half:h0 + head_dim]
        outs.append(t1 * cos - t2 * sin)
        outs.append(t1 * sin + t2 * cos)
    out = jnp.concatenate(outs, axis=1)
    return out * scale if scale != 1.0 else out


def _ep_rotary_roll(accs, extras, *, head_dim, scale):
    acc = accs[0]
    cosf = extras[0][...]
    sins = extras[1][...]
    outs = []
    for h0 in range(0, acc.shape[1], head_dim):
        t = acc[:, h0:h0 + head_dim]
        outs.append(t * cosf + pltpu.roll(t, head_dim // 2, axis=1) * sins)
    out = jnp.concatenate(outs, axis=1)
    return out * scale if scale != 1.0 else out


def ffn(xn, w_in, w_out, lead):
    hid = matmul(xn, w_in, lead, col_offsets=(0, D_FF), n_cols=D_FF,
                 epilogue=_ep_swiglu, out_dtype=BF16, name="ffn_in")
    return matmul(hid, w_out, lead, col_offsets=(0,), n_cols=D_MODEL,
                  epilogue=_ep_plain, out_dtype=BF16, name="ffn_out")


_NT = (((1,), (1,)), ((), ()))
_TN = (((0,), (0,)), ((), ()))


def _retention_kernel(lg_ref, q_ref, k_ref, v_ref, sg_ref, gn_ref, o_ref, state_ref, *, seq):
    c = RET_CHUNK
    dk = RET_DK
    lg = lg_ref[0]
    row = lax.broadcasted_iota(jnp.int32, (c, c), 0)
    col = lax.broadcasted_iota(jnp.int32, (c, c), 1)
    dist = (row - col).astype(F32)
    intra_decay = jnp.where(dist >= 0, jnp.exp(lg[:, :c] * jnp.maximum(dist, 0.0)), 0.0)
    idx = lax.broadcasted_iota(jnp.int32, (c, dk), 0).astype(F32)
    q_decay = jnp.exp(lg * (idx + 1.0))
    k_decay = jnp.exp(lg * (c - 1.0 - idx))
    chunk_decay = jnp.exp(lg * float(c))
    gn = gn_ref[...]
    state_ref[...] = jnp.zeros_like(state_ref)

    def chunk(n, carry):
        rows = pl.ds(pl.multiple_of(n * c, c), c)
        qc = q_ref[rows, :]
        kc = k_ref[rows, :]
        vc = v_ref[rows, :]
        scores = lax.dot_general(qc, kc, _NT, preferred_element_type=F32) * intra_decay
        intra = jnp.dot(scores.astype(BF16), vc, preferred_element_type=F32)
        state = state_ref[...]
        qd = (qc.astype(F32) * q_decay).astype(BF16)
        cross = jnp.dot(qd, state.astype(BF16), preferred_element_type=F32)
        kd = (kc.astype(F32) * k_decay).astype(BF16)
        state_ref[...] = state * chunk_decay + lax.dot_general(
            kd, vc, _TN, preferred_element_type=F32)
        o = intra + cross
        mu = jnp.mean(o, axis=-1, keepdims=True)
        var = jnp.mean(jnp.square(o - mu), axis=-1, keepdims=True)
        on = (o - mu) * lax.rsqrt(var + EPS) * gn
        o_ref[rows, :] = (sg_ref[rows, :].astype(F32) * on).astype(o_ref.dtype)
        return carry

    lax.fori_loop(0, seq // c, chunk, 0, unroll=RET_UNROLL)


def retention_core(q, k, v, sg, gn_g, batch, seq):
    m, d = q.shape
    log_gamma = jnp.log1p(-jnp.exp2(-5.0 - jnp.arange(RET_HEADS, dtype=F32)))
    lg = jnp.broadcast_to(log_gamma[:, None, None], (RET_HEADS, 1, RET_DK))
    blk = pl.BlockSpec((seq, RET_DK), lambda b, h: (b, h))
    return pl.pallas_call(
        functools.partial(_retention_kernel, seq=seq),
        out_shape=jax.ShapeDtypeStruct((m, d), BF16),
        grid=(batch, RET_HEADS),
        in_specs=[pl.BlockSpec((1, 1, RET_DK), lambda b, h: (h, 0, 0)),
                  blk, blk, blk, blk,
                  pl.BlockSpec((1, RET_DK), lambda b, h: (0, h))],
        out_specs=blk,
        scratch_shapes=[pltpu.VMEM((RET_DK, RET_DK), F32)],
        compiler_params=_params(2),
        name="retention_core",
    )(lg, q, k, v, sg, gn_g.reshape(1, d))


def retention_mixer(xn, cos, sin, w_in, gn_g, w_out, j, batch, seq):
    d = D_MODEL
    rot = lambda scale: functools.partial(_ep_rotary_split, head_dim=RET_DK, scale=scale)
    q = matmul(xn, w_in, (j,), col_offsets=(0,), n_cols=d, epilogue=rot(1.0),
               out_dtype=BF16, row_extras=(cos, sin), name="ret_q")
    k = matmul(xn, w_in, (j,), col_offsets=(d,), n_cols=d, epilogue=rot(RET_DK ** -0.5),
               out_dtype=BF16, row_extras=(cos, sin), name="ret_k")
    v = matmul(xn, w_in, (j,), col_offsets=(2 * d,), n_cols=d, epilogue=_ep_plain,
               out_dtype=BF16, name="ret_v")
    sg = matmul(xn, w_in, (j,), col_offsets=(3 * d,), n_cols=d, epilogue=_ep_silu,
                out_dtype=BF16, name="ret_g")
    o = retention_core(q, k, v, sg, gn_g[j], batch, seq)
    return matmul(o, w_out, (j,), col_offsets=(0,), n_cols=d, epilogue=_ep_plain,
                  out_dtype=BF16, name="ret_out")


def _sgu_kernel(u_ref, v_ref, lng_ref, ws_ref, bs_ref, o_ref):
    c = SG_CHUNK
    v = v_ref[...].astype(F32)
    mu = jnp.mean(v, axis=-1, keepdims=True)
    var = jnp.mean(jnp.square(v - mu), axis=-1, keepdims=True)
    vn = ((v - mu) * lax.rsqrt(var + EPS) * lng_ref[...]).astype(BF16)
    row = lax.broadcasted_iota(jnp.int32, (c, c), 0)
    col = lax.broadcasted_iota(jnp.int32, (c, c), 1)
    causal = row >= col
    bs = bs_ref[...]
    for g in range(SG_GROUPS):
        cols = slice(g * SG_DG, (g + 1) * SG_DG)
        w_m = jnp.where(causal, ws_ref[g], 0.0).astype(BF16)
        mixed = jnp.dot(w_m, vn[:, cols], preferred_element_type=F32) + bs[:, g:g + 1]
        o_ref[:, cols] = (u_ref[:, cols].astype(F32) * mixed).astype(o_ref.dtype)


def sgu_core(uv, ln_g, w_s, b_s):
    m = uv.shape[0]
    d = D_MODEL
    c = SG_CHUNK
    return pl.pallas_call(
        _sgu_kernel,
        out_shape=jax.ShapeDtypeStruct((m, d), BF16),
        grid=(m // c,),
        in_specs=[pl.BlockSpec((c, d), lambda n: (n, 0)),
                  pl.BlockSpec((c, d), lambda n: (n, 1)),
                  pl.BlockSpec((1, d), lambda n: (0, 0)),
                  pl.BlockSpec((SG_GROUPS, c, c), lambda n: (0, 0, 0)),
                  pl.BlockSpec((c, SG_GROUPS), lambda n: (0, 0))],
        out_specs=pl.BlockSpec((c, d), lambda n: (n, 0)),
        compiler_params=_params(1),
        name="sgu_core",
    )(uv, uv, ln_g.reshape(1, d), w_s, b_s.T)


def spatial_gating_mixer(xn, w_in, ln_g, w_s, b_s, w_out, j):
    d = D_MODEL
    uv = matmul(xn, w_in, (j,), col_offsets=(0,), n_cols=2 * d, epilogue=_ep_gelu,
                out_dtype=BF16, name="sg_in")
    o = sgu_core(uv, ln_g[j], w_s[j], b_s[j])
    return matmul(o, w_out, (j,), col_offsets=(0,), n_cols=d, epilogue=_ep_plain,
                  out_dtype=BF16, name="sg_out")


MOBA_MASK = -1e30
SUBLANES = 8


def _moba_kernel(q_ref, k_ref, v_ref, o_ref, kaug_ref, vaug_ref, *, seq):
    blk = MOBA_BLOCK
    dh = MOBA_DH
    nblk = seq // blk
    n_sel = min(MOBA_TOPK, nblk - 1)
    pad_rows = 2 * SUBLANES

    vaug_ref[:, :dh] = v_ref[...]
    vaug_ref[:, dh:] = jnp.ones((seq, dh), vaug_ref.dtype)
    kaug_ref[:, :dh] = k_ref[...]
    key_blk = lax.broadcasted_iota(jnp.int32, (seq, dh), 0) // blk
    key_lane = lax.broadcasted_iota(jnp.int32, (seq, dh), 1)
    kaug_ref[:, dh:] = jnp.where(key_blk == key_lane, 1.0, 0.0).astype(kaug_ref.dtype)

    blk_row = lax.broadcasted_iota(jnp.int32, (pad_rows, seq), 0)
    blk_col = lax.broadcasted_iota(jnp.int32, (pad_rows, seq), 1) // blk
    member = jnp.where(blk_row == blk_col, 1.0, 0.0).astype(BF16)
    k_mean = jnp.dot(member, k_ref[...], preferred_element_type=F32) * (1.0 / blk)
    k_mean_hi = k_mean.astype(BF16)
    k_mean_lo = (k_mean - k_mean_hi.astype(F32)).astype(BF16)
    q_all = q_ref[...]
    gate_t = (lax.dot_general(k_mean_hi, q_all, _NT, preferred_element_type=F32)
              + lax.dot_general(k_mean_lo, q_all, _NT, preferred_element_type=F32))

    row = lax.broadcasted_iota(jnp.int32, (blk, LANE), 0)
    lane = lax.broadcasted_iota(jnp.int32, (blk, LANE), 1)
    blk_id = lax.broadcasted_iota(jnp.int32, (SUBLANES, blk), 0)

    for i in range(nblk):
        q = q_ref[i * blk:(i + 1) * blk, :]
        nk = (i + 1) * blk
        if i > n_sel:
            g = gate_t[:SUBLANES, i * blk:(i + 1) * blk]
            bias = jnp.zeros((SUBLANES, blk), F32)
            for j in range(i):
                gj = g[j:j + 1, :]
                beats = ((g > gj) | ((g == gj) & (blk_id < j))) & (blk_id < i)
                rank = jnp.sum(jnp.where(beats, 1.0, 0.0), axis=0, keepdims=True)
                bias = jnp.where((blk_id == j) & (rank >= float(n_sel)), MOBA_MASK, bias)
            bias_t = jnp.concatenate([bias, jnp.zeros((LANE - SUBLANES, blk), F32)], axis=0)
            q_aug = jnp.concatenate([q, bias_t.T.astype(BF16)], axis=1)
            s = lax.dot_general(q_aug, kaug_ref[0:nk, :], _NT, preferred_element_type=F32)
        else:
            s = lax.dot_general(q, k_ref[0:nk, :], _NT, preferred_element_type=F32)
        chunks = [s[:, c * LANE:(c + 1) * LANE] for c in range(nk // LANE)]
        for t, c in enumerate(range(i * blk // LANE, nk // LANE)):
            chunks[c] = jnp.where(lane + t * LANE <= row, chunks[c], float("-inf"))
        m_el = chunks[0]
        for ch in chunks[1:]:
            m_el = jnp.maximum(m_el, ch)
        m = m_el.max(axis=-1, keepdims=True)
        p = jnp.concatenate([jnp.exp(ch - m).astype(BF16) for ch in chunks], axis=1)
        o_aug = jnp.dot(p, vaug_ref[0:nk, :], preferred_element_type=F32)
        o_ref[i * blk:(i + 1) * blk, :] = (o_aug[:, :dh] / o_aug[:, dh:]).astype(o_ref.dtype)


def moba_core(q, k, v, batch, seq):
    m, d = q.shape
    assert seq % MOBA_BLOCK == 0 and MOBA_DH == LANE and seq // MOBA_BLOCK <= SUBLANES
    blk = pl.BlockSpec((seq, MOBA_DH), lambda b, h: (b, h))
    return pl.pallas_call(
        functools.partial(_moba_kernel, seq=seq),
        out_shape=jax.ShapeDtypeStruct((m, d), BF16),
        grid=(batch, d // MOBA_DH),
        in_specs=[blk, blk, blk],
        out_specs=blk,
        scratch_shapes=[pltpu.VMEM((seq, 2 * MOBA_DH), BF16),
                        pltpu.VMEM((seq, 2 * MOBA_DH), BF16)],
        compiler_params=_params(2),
        name="moba_core",
    )(q, k, v)


def moba_mixer(xn, cosf, sins, w_in, w_out, j, batch, seq):
    d = D_MODEL
    rot = lambda scale: functools.partial(_ep_rotary_roll, head_dim=MOBA_DH, scale=scale)
    q = matmul(xn, w_in, (j,), col_offsets=(0,), n_cols=d, epilogue=rot(MOBA_DH ** -0.5),
               out_dtype=BF16, row_extras=(cosf, sins), name="moba_q")
    k = matmul(xn, w_in, (j,), col_offsets=(d,), n_cols=d, epilogue=rot(1.0),
               out_dtype=BF16, row_extras=(cosf, sins), name="moba_k")
    v = matmul(xn, w_in, (j,), col_offsets=(2 * d,), n_cols=d, epilogue=_ep_plain,
               out_dtype=BF16, name="moba_v")
    o = moba_core(q, k, v, batch, seq)
    return matmul(o, w_out, (j,), col_offsets=(0,), n_cols=d, epilogue=_ep_plain,
                  out_dtype=BF16, name="moba_out")


def kernel(x, positions, norm_g, ffn_w_in, ffn_w_out, ret_w_in, ret_gn_g, ret_w_out,
           sg_w_in, sg_ln_g, sg_w_s, sg_b, sg_w_out, moba_w_in, moba_w_out):
    batch, seq, d = x.shape
    m = batch * seq
    pos = positions.reshape(m, 1).astype(F32)

    ret_inv = 1.0 / (10000.0 ** jnp.linspace(0.0, 1.0, RET_DK // 2, dtype=F32))
    ret_ang = pos * ret_inv
    ret_cos, ret_sin = jnp.cos(ret_ang), jnp.sin(ret_ang)
    moba_inv = 1.0 / (ROPE_THETA ** (jnp.arange(0, MOBA_DH, 2, dtype=F32) / MOBA_DH))
    moba_ang = pos * moba_inv
    mc, ms = jnp.cos(moba_ang), jnp.sin(moba_ang)
    moba_cosf = jnp.concatenate([mc, mc], axis=1)
    moba_sins = jnp.concatenate([-ms, ms], axis=1)

    h = x.reshape(m, d)
    xn = pre_norm(h, norm_g[0, 0])
    for i in range(DEPTH):
        mix, j = i % N_MIXERS, i // N_MIXERS
        g = norm_g[i]
        y = ffn(xn, ffn_w_in, ffn_w_out, (i, 0))
        h, xn = post_norm(y, h, g[1], g[2], 0.5)
        if mix == 0:
            y = retention_mixer(xn, ret_cos, ret_sin, ret_w_in, ret_gn_g, ret_w_out,
                                j, batch, seq)
        elif mix == 1:
            y = spatial_gating_mixer(xn, sg_w_in, sg_ln_g, sg_w_s, sg_b, sg_w_out, j)
        else:
            y = moba_mixer(xn, moba_cosf, moba_sins, moba_w_in, moba_w_out, j, batch, seq)
        h, xn = post_norm(y, h, g[3], g[4], 1.0)
        y = ffn(xn, ffn_w_in, ffn_w_out, (i, 1))
        g_next = norm_g[i + 1, 0] if i + 1 < DEPTH else None
        h, xn = post_norm(y, h, g[5], g_next, 0.5)
    return h.reshape(batch, seq, d)
```

```python
import functools
import math

import jax
import jax.numpy as jnp
from jax import lax
from jax.experimental import pallas as pl
from jax.experimental.pallas import tpu as pltpu

F32 = jnp.float32
BF16 = jnp.bfloat16

D_MODEL = 4096
D_FF = 5632
DEPTH = 4
N_MIXERS = 3
RET_HEADS = 16
RET_DK = D_MODEL // RET_HEADS
RET_CHUNK = 128
RET_UNROLL = 8
SG_GROUPS = 16
SG_CHUNK = 128
SG_DG = D_MODEL // SG_GROUPS
MOBA_HEADS = 32
MOBA_DH = D_MODEL // MOBA_HEADS
MOBA_BLOCK = 256
MOBA_TOPK = 3
ROPE_THETA = 10000.0
EPS = 1e-6

LANE = 128
VMEM_LIMIT = 56 * 1024 * 1024
ROW_TILE = 256
MM_TM = 2048
MM_W_BLOCK_BYTES = 8 * 1024 * 1024


def _params(n_grid):
    return pltpu.CompilerParams(
        dimension_semantics=("arbitrary",) * n_grid, vmem_limit_bytes=VMEM_LIMIT)


def _rms(x, g):
    return x * lax.rsqrt(jnp.mean(x * x, axis=-1, keepdims=True) + EPS) * g


def _pre_norm_kernel(h_ref, g_ref, xn_ref):
    xn_ref[...] = _rms(h_ref[...], g_ref[...]).astype(BF16)


def pre_norm(h, g):
    m, d = h.shape
    return pl.pallas_call(
        _pre_norm_kernel,
        out_shape=jax.ShapeDtypeStruct((m, d), BF16),
        grid=(m // ROW_TILE,),
        in_specs=[pl.BlockSpec((ROW_TILE, d), lambda i: (i, 0)),
                  pl.BlockSpec((1, d), lambda i: (0, 0))],
        out_specs=pl.BlockSpec((ROW_TILE, d), lambda i: (i, 0)),
        compiler_params=_params(1),
        name="pre_norm",
    )(h, g.reshape(1, d))


def _post_norm_kernel(y_ref, h_ref, gpost_ref, gpre_ref, hout_ref, xn_ref, *, coef):
    h = h_ref[...] + coef * _rms(y_ref[...].astype(F32), gpost_ref[...])
    hout_ref[...] = h
    xn_ref[...] = _rms(h, gpre_ref[...]).astype(BF16)


def _post_norm_last_kernel(y_ref, h_ref, gpost_ref, hout_ref, *, coef):
    hout_ref[...] = h_ref[...] + coef * _rms(y_ref[...].astype(F32), gpost_ref[...])


def post_norm(y, h, g_post, g_pre, coef):
    m, d = h.shape
    row = pl.BlockSpec((ROW_TILE, d), lambda i: (i, 0))
    vec = pl.BlockSpec((1, d), lambda i: (0, 0))
    if g_pre is None:
        return pl.pallas_call(
            functools.partial(_post_norm_last_kernel, coef=coef),
            out_shape=jax.ShapeDtypeStruct((m, d), F32),
            grid=(m // ROW_TILE,),
            in_specs=[row, row, vec],
            out_specs=row,
            compiler_params=_params(1),
            name="post_norm_last",
        )(y, h, g_post.reshape(1, d)), None
    return pl.pallas_call(
        functools.partial(_post_norm_kernel, coef=coef),
        out_shape=(jax.ShapeDtypeStruct((m, d), F32), jax.ShapeDtypeStruct((m, d), BF16)),
        grid=(m // ROW_TILE,),
        in_specs=[row, row, vec, vec],
        out_specs=(row, row),
        compiler_params=_params(1),
        name="post_norm",
    )(y, h, g_post.reshape(1, d), g_pre.reshape(1, d))


def _mm_kernel(*refs, n_parts, n_extra, epilogue):
    x_ref = refs[0]
    w_refs = refs[1:1 + n_parts]
    extra_refs = refs[1 + n_parts:1 + n_parts + n_extra]
    out_ref = refs[1 + n_parts + n_extra]
    x = x_ref[...]
    accs = [jnp.dot(x, w_ref[...].astype(BF16), preferred_element_type=F32) for w_ref in w_refs]
    out_ref[...] = epilogue(accs, extra_refs).astype(out_ref.dtype)


def _mm_col_tile(k, n_parts):
    tn = LANE
    while n_parts * k * (2 * tn) * 4 <= MM_W_BLOCK_BYTES:
        tn *= 2
    return tn


def matmul(x, w, lead, *, col_offsets, n_cols, epilogue, out_dtype, row_extras=(), name):
    m, k = x.shape
    tm = MM_TM
    tn = _mm_col_tile(k, len(col_offsets))
    assert m % tm == 0 and n_cols % tn == 0 and all(o % tn == 0 for o in col_offsets)
    assert w.shape[len(lead)] == k
    in_specs = [pl.BlockSpec((tm, k), lambda i, j: (i, 0), pipeline_mode=pl.Buffered(1))]
    w_block = (None,) * len(lead) + (k, tn)
    for off in col_offsets:
        in_specs.append(pl.BlockSpec(w_block, lambda i, j, ob=off // tn: (*lead, 0, ob + j)))
    for e in row_extras:
        in_specs.append(pl.BlockSpec((tm, e.shape[1]), lambda i, j: (i, 0)))
    kern = functools.partial(_mm_kernel, n_parts=len(col_offsets), n_extra=len(row_extras),
                             epilogue=epilogue)
    return pl.pallas_call(
        kern,
        out_shape=jax.ShapeDtypeStruct((m, n_cols), out_dtype),
        grid=(m // tm, n_cols // tn),
        in_specs=in_specs,
        out_specs=pl.BlockSpec((tm, tn), lambda i, j: (i, j)),
        compiler_params=_params(2),
        name=name,
    )(x, *([w] * len(col_offsets)), *row_extras)


def _ep_plain(accs, extras):
    return accs[0]


def _ep_swiglu(accs, extras):
    a, b = accs
    return a * jax.nn.sigmoid(a) * b


def _ep_silu(accs, extras):
    a = accs[0]
    return a * jax.nn.sigmoid(a)


def _ep_gelu(accs, extras):
    a = accs[0]
    return 0.5 * a * (1.0 + lax.erf(a * math.sqrt(0.5)))


def _ep_rotary_split(accs, extras, *, head_dim, scale):
    acc = accs[0]
    cos = extras[0][...]
    sin = extras[1][...]
    half = head_dim // 2
    outs = []
    for h0 in range(0, acc.shape[1], head_dim):
        t1 = acc[:, h0:h0 + half]
        t2 = acc[:, h0 + half:h0 + head_dim]
        outs.append(t1 * cos - t2 * sin)
        outs.append(t1 * sin + t2 * cos)
    out = jnp.concatenate(outs, axis=1)
    return out * scale if scale != 1.0 else out


def _ep_rotary_roll(accs, extras, *, head_dim, scale):
    acc = accs[0]
    cosf = extras[0][...]
    sins = extras[1][...]
    outs = []
    for h0 in range(0, acc.shape[1], head_dim):
        t = acc[:, h0:h0 + head_dim]
        outs.append(t * cosf + pltpu.roll(t, head_dim // 2, axis=1) * sins)
    out = jnp.concatenate(outs, axis=1)
    return out * scale if scale != 1.0 else out


def ffn(xn, w_in, w_out, lead):
    hid = matmul(xn, w_in, lead, col_offsets=(0, D_FF), n_cols=D_FF,
                 epilogue=_ep_swiglu, out_dtype=BF16, name="ffn_in")
    return matmul(hid, w_out, lead, col_offsets=(0,), n_cols=D_MODEL,
                  epilogue=_ep_plain, out_dtype=BF16, name="ffn_out")


_NT = (((1,), (1,)), ((), ()))
_TN = (((0,), (0,)), ((), ()))


def _retention_kernel(lg_ref, q_ref, k_ref, v_ref, sg_ref, gn_ref, o_ref, state_ref, *, seq):
    c = RET_CHUNK
    dk = RET_DK
    lg = lg_ref[0]
    row = lax.broadcasted_iota(jnp.int32, (c, c), 0)
    col = lax.broadcasted_iota(jnp.int32, (c, c), 1)
    dist = (row - col).astype(F32)
    intra_decay = jnp.where(dist >= 0, jnp.exp(lg[:, :c] * jnp.maximum(dist, 0.0)), 0.0)
    idx = lax.broadcasted_iota(jnp.int32, (c, dk), 0).astype(F32)
    q_decay = jnp.exp(lg * (idx + 1.0))
    k_decay = jnp.exp(lg * (c - 1.0 - idx))
    chunk_decay = jnp.exp(lg * float(c))
    gn = gn_ref[...]
    state_ref[...] = jnp.zeros_like(state_ref)

    def chunk(n, carry):
        rows = pl.ds(pl.multiple_of(n * c, c), c)
        qc = q_ref[rows, :]
        kc = k_ref[rows, :]
        vc = v_ref[rows, :]
        scores = lax.dot_general(qc, kc, _NT, preferred_element_type=F32) * intra_decay
        intra = jnp.dot(scores.astype(BF16), vc, preferred_element_type=F32)
        state = state_ref[...]
        qd = (qc.astype(F32) * q_decay).astype(BF16)
        cross = jnp.dot(qd, state.astype(BF16), preferred_element_type=F32)
        kd = (kc.astype(F32) * k_decay).astype(BF16)
        state_ref[...] = state * chunk_decay + lax.dot_general(
            kd, vc, _TN, preferred_element_type=F32)
        o = intra + cross
        mu = jnp.mean(o, axis=-1, keepdims=True)
        var = jnp.mean(jnp.square(o - mu), axis=-1, keepdims=True)
        on = (o - mu) * lax.rsqrt(var + EPS) * gn
        o_ref[rows, :] = (sg_ref[rows, :].astype(F32) * on).astype(o_ref.dtype)
        return carry

    lax.fori_loop(0, seq // c, chunk, 0, unroll=RET_UNROLL)


def retention_core(q, k, v, sg, gn_g, batch, seq):
    m, d = q.shape
    log_gamma = jnp.log1p(-jnp.exp2(-5.0 - jnp.arange(RET_HEADS, dtype=F32)))
    lg = jnp.broadcast_to(log_gamma[:, None, None], (RET_HEADS, 1, RET_DK))
    blk = pl.BlockSpec((seq, RET_DK), lambda b, h: (b, h))
    return pl.pallas_call(
        functools.partial(_retention_kernel, seq=seq),
        out_shape=jax.ShapeDtypeStruct((m, d), BF16),
        grid=(batch, RET_HEADS),
        in_specs=[pl.BlockSpec((1, 1, RET_DK), lambda b, h: (h, 0, 0)),
                  blk, blk, blk, blk,
                  pl.BlockSpec((1, RET_DK), lambda b, h: (0, h))],
        out_specs=blk,
        scratch_shapes=[pltpu.VMEM((RET_DK, RET_DK), F32)],
        compiler_params=_params(2),
        name="retention_core",
    )(lg, q, k, v, sg, gn_g.reshape(1, d))


def retention_mixer(xn, cos, sin, w_in, gn_g, w_out, j, batch, seq):
    d = D_MODEL
    rot = lambda scale: functools.partial(_ep_rotary_split, head_dim=RET_DK, scale=scale)
    q = matmul(xn, w_in, (j,), col_offsets=(0,), n_cols=d, epilogue=rot(1.0),
               out_dtype=BF16, row_extras=(cos, sin), name="ret_q")
    k = matmul(xn, w_in, (j,), col_offsets=(d,), n_cols=d, epilogue=rot(RET_DK ** -0.5),
               out_dtype=BF16, row_extras=(cos, sin), name="ret_k")
    v = matmul(xn, w_in, (j,), col_offsets=(2 * d,), n_cols=d, epilogue=_ep_plain,
               out_dtype=BF16, name="ret_v")
    sg = matmul(xn, w_in, (j,), col_offsets=(3 * d,), n_cols=d, epilogue=_ep_silu,
                out_dtype=BF16, name="ret_g")
    o = retention_core(q, k, v, sg, gn_g[j], batch, seq)
    return matmul(o, w_out, (j,), col_offsets=(0,), n_cols=d, epilogue=_ep_plain,
                  out_dtype=BF16, name="ret_out")


def _sgu_kernel(u_ref, v_ref, lng_ref, ws_ref, bs_ref, o_ref):
    c = SG_CHUNK
    v = v_ref[...].astype(F32)
    mu = jnp.mean(v, axis=-1, keepdims=True)
    var = jnp.mean(jnp.square(v - mu), axis=-1, keepdims=True)
    vn = ((v - mu) * lax.rsqrt(var + EPS) * lng_ref[...]).astype(BF16)
    row = lax.broadcasted_iota(jnp.int32, (c, c), 0)
    col = lax.broadcasted_iota(jnp.int32, (c, c), 1)
    causal = row >= col
    bs = bs_ref[...]
    for g in range(SG_GROUPS):
        cols = slice(g * SG_DG, (g + 1) * SG_DG)
        w_m = jnp.where(causal, ws_ref[g], 0.0).astype(BF16)
        mixed = jnp.dot(w_m, vn[:, cols], preferred_element_type=F32) + bs[:, g:g + 1]
        o_ref[:, cols] = (u_ref[:, cols].astype(F32) * mixed).astype(o_ref.dtype)


def sgu_core(uv, ln_g, w_s, b_s):
    m = uv.shape[0]
    d = D_MODEL
    c = SG_CHUNK
    return pl.pallas_call(
        _sgu_kernel,
        out_shape=jax.ShapeDtypeStruct((m, d), BF16),
        grid=(m // c,),
        in_specs=[pl.BlockSpec((c, d), lambda n: (n, 0)),
                  pl.BlockSpec((c, d), lambda n: (n, 1)),
                  pl.BlockSpec((1, d), lambda n: (0, 0)),
                  pl.BlockSpec((SG_GROUPS, c, c), lambda n: (0, 0, 0)),
                  pl.BlockSpec((c, SG_GROUPS), lambda n: (0, 0))],
        out_specs=pl.BlockSpec((c, d), lambda n: (n, 0)),
        compiler_params=_params(1),
        name="sgu_core",
    )(uv, uv, ln_g.reshape(1, d), w_s, b_s.T)


def spatial_gating_mixer(xn, w_in, ln_g, w_s, b_s, w_out, j):
    d = D_MODEL
    uv = matmul(xn, w_in, (j,), col_offsets=(0,), n_cols=2 * d, epilogue=_ep_gelu,
                out_dtype=BF16, name="sg_in")
    o = sgu_core(uv, ln_g[j], w_s[j], b_s[j])
    return matmul(o, w_out, (j,), col_offsets=(0,), n_cols=d, epilogue=_ep_plain,
                  out_dtype=BF16, name="sg_out")


MOBA_MASK = -1e30
SUBLANES = 8


def _moba_kernel(q_ref, k_ref, v_ref, o_ref, kaug_ref, vaug_ref, *, seq):
    blk = MOBA_BLOCK
    dh = MOBA_DH
    nblk = seq // blk
    n_sel = min(MOBA_TOPK, nblk - 1)
    pad_rows = 2 * SUBLANES

    vaug_ref[:, :dh] = v_ref[...]
    vaug_ref[:, dh:] = jnp.ones((seq, dh), vaug_ref.dtype)
    kaug_ref[:, :dh] = k_ref[...]
    key_blk = lax.broadcasted_iota(jnp.int32, (seq, dh), 0) // blk
    key_lane = lax.broadcasted_iota(jnp.int32, (seq, dh), 1)
    kaug_ref[:, dh:] = jnp.where(key_blk == key_lane, 1.0, 0.0).astype(kaug_ref.dtype)

    blk_row = lax.broadcasted_iota(jnp.int32, (pad_rows, seq), 0)
    blk_col = lax.broadcasted_iota(jnp.int32, (pad_rows, seq), 1) // blk
    member = jnp.where(blk_row == blk_col, 1.0, 0.0).astype(BF16)
    k_mean = jnp.dot(member, k_ref[...], preferred_element_type=F32) * (1.0 / blk)
    k_mean_hi = k_mean.astype(BF16)
    k_mean_lo = (k_mean - k_mean_hi.astype(F32)).astype(BF16)
    q_all = q_ref[...]
    gate_t = (lax.dot_general(k_mean_hi, q_all, _NT, preferred_element_type=F32)
              + lax.dot_general(k_mean_lo, q_all, _NT, preferred_element_type=F32))

    row = lax.broadcasted_iota(jnp.int32, (blk, LANE), 0)
    lane = lax.broadcasted_iota(jnp.int32, (blk, LANE), 1)
    blk_id = lax.broadcasted_iota(jnp.int32, (SUBLANES, blk), 0)

    for i in range(nblk):
        q = q_ref[i * blk:(i + 1) * blk, :]
        nk = (i + 1) * blk
        if i > n_sel:
            g = gate_t[:SUBLANES, i * blk:(i + 1) * blk]
            bias = jnp.zeros((SUBLANES, blk), F32)
            for j in range(i):
                gj = g[j:j + 1, :]
                beats = ((g > gj) | ((g == gj) & (blk_id < j))) & (blk_id < i)
                rank = jnp.sum(jnp.where(beats, 1.0, 0.0), axis=0, keepdims=True)
                bias = jnp.where((blk_id == j) & (rank >= float(n_sel)), MOBA_MASK, bias)
            bias_t = jnp.concatenate([bias, jnp.zeros((LANE - SUBLANES, blk), F32)], axis=0)
            q_aug = jnp.concatenate([q, bias_t.T.astype(BF16)], axis=1)
            s = lax.dot_general(q_aug, kaug_ref[0:nk, :], _NT, preferred_element_type=F32)
        else:
            s = lax.dot_general(q, k_ref[0:nk, :], _NT, preferred_element_type=F32)
        chunks = [s[:, c * LANE:(c + 1) * LANE] for c in range(nk // LANE)]
        for t, c in enumerate(range(i * blk // LANE, nk // LANE)):
            chunks[c] = jnp.where(lane + t * LANE <= row, chunks[c], float("-inf"))
        m_el = chunks[0]
        for ch in chunks[1:]:
            m_el = jnp.maximum(m_el, ch)
        m = m_el.max(axis=-1, keepdims=True)
        p = jnp.concatenate([jnp.exp(ch - m).astype(BF16) for ch in chunks], axis=1)
        o_aug = jnp.dot(p, vaug_ref[0:nk, :], preferred_element_type=F32)
        o_ref[i * blk:(i + 1) * blk, :] = (o_aug[:, :dh] / o_aug[:, dh:]).astype(o_ref.dtype)


def moba_core(q, k, v, batch, seq):
    m, d = q.shape
    assert seq % MOBA_BLOCK == 0 and MOBA_DH == LANE and seq // MOBA_BLOCK <= SUBLANES
    blk = pl.BlockSpec((seq, MOBA_DH), lambda b, h: (b, h))
    return pl.pallas_call(
        functools.partial(_moba_kernel, seq=seq),
        out_shape=jax.ShapeDtypeStruct((m, d), BF16),
        grid=(batch, d // MOBA_DH),
        in_specs=[blk, blk, blk],
        out_specs=blk,
        scratch_shapes=[pltpu.VMEM((seq, 2 * MOBA_DH), BF16),
                        pltpu.VMEM((seq, 2 * MOBA_DH), BF16)],
        compiler_params=_params(2),
        name="moba_core",
    )(q, k, v)


def moba_mixer(xn, cosf, sins, w_in, w_out, j, batch, seq):
    d = D_MODEL
    rot = lambda scale: functools.partial(_ep_rotary_roll, head_dim=MOBA_DH, scale=scale)
    q = matmul(xn, w_in, (j,), col_offsets=(0,), n_cols=d, epilogue=rot(MOBA_DH ** -0.5),
               out_dtype=BF16, row_extras=(cosf, sins), name="moba_q")
    k = matmul(xn, w_in, (j,), col_offsets=(d,), n_cols=d, epilogue=rot(1.0),
               out_dtype=BF16, row_extras=(cosf, sins), name="moba_k")
    v = matmul(xn, w_in, (j,), col_offsets=(2 * d,), n_cols=d, epilogue=_ep_plain,
               out_dtype=BF16, name="moba_v")
    o = moba_core(q, k, v, batch, seq)
    return matmul(o, w_out, (j,), col_offsets=(0,), n_cols=d, epilogue=_ep_plain,
                  out_dtype=BF16, name="moba_out")


def kernel(x, positions, norm_g, ffn_w_in, ffn_w_out, ret_w_in, ret_gn_g, ret_w_out,
           sg_w_in, sg_ln_g, sg_w_s, sg_b, sg_w_out, moba_w_in, moba_w_out):
    batch, seq, d = x.shape
    m = batch * seq
    pos = positions.reshape(m, 1).astype(F32)

    ret_inv = 1.0 / (10000.0 ** jnp.linspace(0.0, 1.0, RET_DK // 2, dtype=F32))
    ret_ang = pos * ret_inv
    ret_cos, ret_sin = jnp.cos(ret_ang), jnp.sin(ret_ang)
    moba_inv = 1.0 / (ROPE_THETA ** (jnp.arange(0, MOBA_DH, 2, dtype=F32) / MOBA_DH))
    moba_ang = pos * moba_inv
    mc, ms = jnp.cos(moba_ang), jnp.sin(moba_ang)
    moba_cosf = jnp.concatenate([mc, mc], axis=1)
    moba_sins = jnp.concatenate([-ms, ms], axis=1)

    h = x.reshape(m, d)
    xn = pre_norm(h, norm_g[0, 0])
    for i in range(DEPTH):
        mix, j = i % N_MIXERS, i // N_MIXERS
        g = norm_g[i]
        y = ffn(xn, ffn_w_in, ffn_w_out, (i, 0))
        h, xn = post_norm(y, h, g[1], g[2], 0.5)
        if mix == 0:
            y = retention_mixer(xn, ret_cos, ret_sin, ret_w_in, ret_gn_g, ret_w_out,
                                j, batch, seq)
        elif mix == 1:
            y = spatial_gating_mixer(xn, sg_w_in, sg_ln_g, sg_w_s, sg_b, sg_w_out, j)
        else:
            y = moba_mixer(xn, moba_cosf, moba_sins, moba_w_in, moba_w_out, j, batch, seq)
        h, xn = post_norm(y, h, g[3], g[4], 1.0)
        y = ffn(xn, ffn_w_in, ffn_w_out, (i, 1))
        g_next = norm_g[i + 1, 0] if i + 1 < DEPTH else None
        h, xn = post_norm(y, h, g[5], g_next, 0.5)
    return h.reshape(batch, seq, d)
```

```python
import functools
import math

import jax
import jax.numpy as jnp
from jax import lax
from jax.experimental import pallas as pl
from jax.experimental.pallas import tpu as pltpu

F32 = jnp.float32
BF16 = jnp.bfloat16

D_MODEL = 4096
D_FF = 5632
DEPTH = 4
N_MIXERS = 3
RET_HEADS = 16
RET_DK = D_MODEL // RET_HEADS
RET_CHUNK = 128
RET_UNROLL = 16
SG_GROUPS = 16
SG_CHUNK = 128
SG_ROWS = 512
SG_DG = D_MODEL // SG_GROUPS
MOBA_HEADS = 32
MOBA_DH = D_MODEL // MOBA_HEADS
MOBA_BLOCK = 256
MOBA_TOPK = 3
ROPE_THETA = 10000.0
EPS = 1e-6

LANE = 128
VMEM_LIMIT = 56 * 1024 * 1024
ROW_TILE = 256
MM_TM = 1024
MM_W_BLOCK_BYTES = 8 * 1024 * 1024


def _params(n_grid):
    return pltpu.CompilerParams(
        dimension_semantics=("arbitrary",) * n_grid, vmem_limit_bytes=VMEM_LIMIT)


def _rms(x, g):
    return x * lax.rsqrt(jnp.mean(x * x, axis=-1, keepdims=True) + EPS) * g


def _pre_norm_kernel(h_ref, g_ref, xn_ref):
    xn_ref[...] = _rms(h_ref[...], g_ref[...]).astype(BF16)


def pre_norm(h, g):
    m, d = h.shape
    return pl.pallas_call(
        _pre_norm_kernel,
        out_shape=jax.ShapeDtypeStruct((m, d), BF16),
        grid=(m // ROW_TILE,),
        in_specs=[pl.BlockSpec((ROW_TILE, d), lambda i: (i, 0)),
                  pl.BlockSpec((1, d), lambda i: (0, 0))],
        out_specs=pl.BlockSpec((ROW_TILE, d), lambda i: (i, 0)),
        compiler_params=_params(1),
        name="pre_norm",
    )(h, g.reshape(1, d))


def _post_norm_kernel(y_ref, h_ref, gpost_ref, gpre_ref, hout_ref, xn_ref, *, coef):
    h = h_ref[...] + coef * _rms(y_ref[...].astype(F32), gpost_ref[...])
    hout_ref[...] = h
    xn_ref[...] = _rms(h, gpre_ref[...]).astype(BF16)


def _post_norm_last_kernel(y_ref, h_ref, gpost_ref, hout_ref, *, coef):
    hout_ref[...] = h_ref[...] + coef * _rms(y_ref[...].astype(F32), gpost_ref[...])


def post_norm(y, h, g_post, g_pre, coef):
    m, d = h.shape
    row = pl.BlockSpec((ROW_TILE, d), lambda i: (i, 0))
    vec = pl.BlockSpec((1, d), lambda i: (0, 0))
    if g_pre is None:
        return pl.pallas_call(
            functools.partial(_post_norm_last_kernel, coef=coef),
            out_shape=jax.ShapeDtypeStruct((m, d), F32),
            grid=(m // ROW_TILE,),
            in_specs=[row, row, vec],
            out_specs=row,
            compiler_params=_params(1),
            name="post_norm_last",
        )(y, h, g_post.reshape(1, d)), None
    return pl.pallas_call(
        functools.partial(_post_norm_kernel, coef=coef),
        out_shape=(jax.ShapeDtypeStruct((m, d), F32), jax.ShapeDtypeStruct((m, d), BF16)),
        grid=(m // ROW_TILE,),
        in_specs=[row, row, vec, vec],
        out_specs=(row, row),
        compiler_params=_params(1),
        name="post_norm",
    )(y, h, g_post.reshape(1, d), g_pre.reshape(1, d))


def _mm_kernel(*refs, n_parts, n_extra, epilogue):
    x_ref = refs[0]
    w_refs = refs[1:1 + n_parts]
    extra_refs = refs[1 + n_parts:1 + n_parts + n_extra]
    out_ref = refs[1 + n_parts + n_extra]
    x = x_ref[...]
    accs = [jnp.dot(x, w_ref[...].astype(BF16), preferred_element_type=F32) for w_ref in w_refs]
    out_ref[...] = epilogue(accs, extra_refs).astype(out_ref.dtype)


def _mm_col_tile(k, n_parts):
    tn = LANE
    while n_parts * k * (2 * tn) * 4 <= MM_W_BLOCK_BYTES:
        tn *= 2
    return tn


def matmul(x, w, lead, *, col_offsets, n_cols, epilogue, out_dtype, row_extras=(), name):
    m, k = x.shape
    tm = MM_TM
    tn = _mm_col_tile(k, len(col_offsets))
    assert m % tm == 0 and n_cols % tn == 0 and all(o % tn == 0 for o in col_offsets)
    assert w.shape[len(lead)] == k
    in_specs = [pl.BlockSpec((tm, k), lambda i, j: (i, 0))]
    w_block = (None,) * len(lead) + (k, tn)
    for off in col_offsets:
        in_specs.append(pl.BlockSpec(w_block, lambda i, j, ob=off // tn: (*lead, 0, ob + j)))
    for e in row_extras:
        in_specs.append(pl.BlockSpec((tm, e.shape[1]), lambda i, j: (i, 0)))
    kern = functools.partial(_mm_kernel, n_parts=len(col_offsets), n_extra=len(row_extras),
                             epilogue=epilogue)
    return pl.pallas_call(
        kern,
        out_shape=jax.ShapeDtypeStruct((m, n_cols), out_dtype),
        grid=(m // tm, n_cols // tn),
        in_specs=in_specs,
        out_specs=pl.BlockSpec((tm, tn), lambda i, j: (i, j)),
        compiler_params=_params(2),
        name=name,
    )(x, *([w] * len(col_offsets)), *row_extras)


def _ep_plain(accs, extras):
    return accs[0]


def _ep_swiglu(accs, extras):
    a, b = accs
    return a * jax.nn.sigmoid(a) * b


def _ep_silu(accs, extras):
    a = accs[0]
    return a * jax.nn.sigmoid(a)


def _ep_gelu(accs, extras):
    a = accs[0]
    return 0.5 * a * (1.0 + lax.erf(a * math.sqrt(0.5)))


def _ep_rotary_split(accs, extras, *, head_dim, scale):
    acc = accs[0]
    cos = extras[0][...]
    sin = extras[1][...]
    half = head_dim // 2
    outs = []
    for h0 in range(0, acc.shape[1], head_dim):
        t1 = acc[:, h0:h0 + half]
        t2 = acc[:, h0 + half:h0 + head_dim]
        outs.append(t1 * cos - t2 * sin)
        outs.append(t1 * sin + t2 * cos)
    out = jnp.concatenate(outs, axis=1)
    return out * scale if scale != 1.0 else out


def _ep_rotary_roll(accs, extras, *, head_dim, scale):
    acc = accs[0]
    cosf = extras[0][...]
    sins = extras[1][...]
    outs = []
    for h0 in range(0, acc.shape[1], head_dim):
        t = acc[:, h0:h0 + head_dim]
        outs.append(t * cosf + pltpu.roll(t, head_dim // 2, axis=1) * sins)
    out = jnp.concatenate(outs, axis=1)
    return out * scale if scale != 1.0 else out


def ffn(xn, w_in, w_out, lead):
    hid = matmul(xn, w_in, lead, col_offsets=(0, D_FF), n_cols=D_FF,
                 epilogue=_ep_swiglu, out_dtype=BF16, name="ffn_in")
    return matmul(hid, w_out, lead, col_offsets=(0,), n_cols=D_MODEL,
                  epilogue=_ep_plain, out_dtype=BF16, name="ffn_out")


_NT = (((1,), (1,)), ((), ()))
_TN = (((0,), (0,)), ((), ()))


def _retention_kernel(lg_ref, q_ref, k_ref, v_ref, sg_ref, gn_ref, o_ref, state_ref, *, seq):
    c = RET_CHUNK
    dk = RET_DK
    lg = lg_ref[0]
    row = lax.broadcasted_iota(jnp.int32, (c, c), 0)
    col = lax.broadcasted_iota(jnp.int32, (c, c), 1)
    dist = (row - col).astype(F32)
    intra_decay = jnp.where(dist >= 0, jnp.exp(lg[:, :c] * jnp.maximum(dist, 0.0)), 0.0)
    idx = lax.broadcasted_iota(jnp.int32, (c, dk), 0).astype(F32)
    q_decay = jnp.exp(lg * (idx + 1.0))
    k_decay = jnp.exp(lg * (c - 1.0 - idx))
    chunk_decay = jnp.exp(lg * float(c))
    gn = gn_ref[...]
    state_ref[...] = jnp.zeros_like(state_ref)

    def front(rows):
        qc = q_ref[rows, :]
        kc = k_ref[rows, :]
        vc = v_ref[rows, :]
        scores = lax.dot_general(qc, kc, _NT, preferred_element_type=F32) * intra_decay
        intra = jnp.dot(scores.astype(BF16), vc, preferred_element_type=F32)
        qd = (qc.astype(F32) * q_decay).astype(BF16)
        kd = (kc.astype(F32) * k_decay).astype(BF16)
        update = lax.dot_general(kd, vc, _TN, preferred_element_type=F32)
        return qd, intra, update

    def back(rows, qd, intra, update):
        state = state_ref[...]
        cross = jnp.dot(qd, state.astype(BF16), preferred_element_type=F32)
        state_ref[...] = state * chunk_decay + update
        o = intra + cross
        mu = jnp.mean(o, axis=-1, keepdims=True)
        var = jnp.mean(jnp.square(o - mu), axis=-1, keepdims=True)
        on = (o - mu) * lax.rsqrt(var + EPS) * gn
        o_ref[rows, :] = (sg_ref[rows, :].astype(F32) * on).astype(o_ref.dtype)

    def group(gi, carry):
        rows = lambda u: pl.ds(pl.multiple_of((gi * RET_UNROLL + u) * c, c), c)
        pending = front(rows(0))
        for u in range(RET_UNROLL):
            current = pending
            if u + 1 < RET_UNROLL:
                pending = front(rows(u + 1))
            back(rows(u), *current)
        return carry

    lax.fori_loop(0, seq // (c * RET_UNROLL), group, 0)


def retention_core(q, k, v, sg, gn_g, batch, seq):
    m, d = q.shape
    log_gamma = jnp.log1p(-jnp.exp2(-5.0 - jnp.arange(RET_HEADS, dtype=F32)))
    lg = jnp.broadcast_to(log_gamma[:, None, None], (RET_HEADS, 1, RET_DK))
    blk = pl.BlockSpec((seq, RET_DK), lambda b, h: (b, h))
    return pl.pallas_call(
        functools.partial(_retention_kernel, seq=seq),
        out_shape=jax.ShapeDtypeStruct((m, d), BF16),
        grid=(batch, RET_HEADS),
        in_specs=[pl.BlockSpec((1, 1, RET_DK), lambda b, h: (h, 0, 0)),
                  blk, blk, blk, blk,
                  pl.BlockSpec((1, RET_DK), lambda b, h: (0, h))],
        out_specs=blk,
        scratch_shapes=[pltpu.VMEM((RET_DK, RET_DK), F32)],
        compiler_params=_params(2),
        name="retention_core",
    )(lg, q, k, v, sg, gn_g.reshape(1, d))


def retention_mixer(xn, cos, sin, w_in, gn_g, w_out, j, batch, seq):
    d = D_MODEL
    rot = lambda scale: functools.partial(_ep_rotary_split, head_dim=RET_DK, scale=scale)
    q = matmul(xn, w_in, (j,), col_offsets=(0,), n_cols=d, epilogue=rot(1.0),
               out_dtype=BF16, row_extras=(cos, sin), name="ret_q")
    k = matmul(xn, w_in, (j,), col_offsets=(d,), n_cols=d, epilogue=rot(RET_DK ** -0.5),
               out_dtype=BF16, row_extras=(cos, sin), name="ret_k")
    v = matmul(xn, w_in, (j,), col_offsets=(2 * d,), n_cols=d, epilogue=_ep_plain,
               out_dtype=BF16, name="ret_v")
    sg = matmul(xn, w_in, (j,), col_offsets=(3 * d,), n_cols=d, epilogue=_ep_silu,
                out_dtype=BF16, name="ret_g")
    o = retention_core(q, k, v, sg, gn_g[j], batch, seq)
    return matmul(o, w_out, (j,), col_offsets=(0,), n_cols=d, epilogue=_ep_plain,
                  out_dtype=BF16, name="ret_out")


def _sgu_kernel(u_ref, v_ref, lng_ref, ws_ref, bs_ref, o_ref, wm_ref):
    c = SG_CHUNK
    row = lax.broadcasted_iota(jnp.int32, (c, c), 0)
    col = lax.broadcasted_iota(jnp.int32, (c, c), 1)
    causal = row >= col
    for g in range(SG_GROUPS):
        wm_ref[g] = jnp.where(causal, ws_ref[g], 0.0).astype(wm_ref.dtype)
    bs = bs_ref[...]
    lng = lng_ref[...]

    def chunk(n, carry):
        rows = pl.ds(pl.multiple_of(n * c, c), c)
        v = v_ref[rows, :].astype(F32)
        mu = jnp.mean(v, axis=-1, keepdims=True)
        var = jnp.mean(jnp.square(v - mu), axis=-1, keepdims=True)
        vn = ((v - mu) * lax.rsqrt(var + EPS) * lng).astype(BF16)
        for g in range(SG_GROUPS):
            cols = slice(g * SG_DG, (g + 1) * SG_DG)
            mixed = jnp.dot(wm_ref[g], vn[:, cols], preferred_element_type=F32) + bs[:, g:g + 1]
            o_ref[rows, cols] = (u_ref[rows, cols].astype(F32) * mixed).astype(o_ref.dtype)
        return carry

    lax.fori_loop(0, SG_ROWS // c, chunk, 0)


def sgu_core(uv, ln_g, w_s, b_s):
    m = uv.shape[0]
    d = D_MODEL
    c = SG_CHUNK
    assert m % SG_ROWS == 0 and SG_ROWS % c == 0
    return pl.pallas_call(
        _sgu_kernel,
        out_shape=jax.ShapeDtypeStruct((m, d), BF16),
        grid=(m // SG_ROWS,),
        in_specs=[pl.BlockSpec((SG_ROWS, d), lambda n: (n, 0)),
                  pl.BlockSpec((SG_ROWS, d), lambda n: (n, 1)),
                  pl.BlockSpec((1, d), lambda n: (0, 0)),
                  pl.BlockSpec((SG_GROUPS, c, c), lambda n: (0, 0, 0)),
                  pl.BlockSpec((c, SG_GROUPS), lambda n: (0, 0))],
        out_specs=pl.BlockSpec((SG_ROWS, d), lambda n: (n, 0)),
        scratch_shapes=[pltpu.VMEM((SG_GROUPS, c, c), BF16)],
        compiler_params=_params(1),
        name="sgu_core",
    )(uv, uv, ln_g.reshape(1, d), w_s, b_s.T)


def spatial_gating_mixer(xn, w_in, ln_g, w_s, b_s, w_out, j):
    d = D_MODEL
    uv = matmul(xn, w_in, (j,), col_offsets=(0,), n_cols=2 * d, epilogue=_ep_gelu,
                out_dtype=BF16, name="sg_in")
    o = sgu_core(uv, ln_g[j], w_s[j], b_s[j])
    return matmul(o, w_out, (j,), col_offsets=(0,), n_cols=d, epilogue=_ep_plain,
                  out_dtype=BF16, name="sg_out")


MOBA_MASK = -1e30
SUBLANES = 8


def _moba_kernel(q_ref, k_ref, v_ref, o_ref, kaug_ref, vaug_ref, *, seq):
    blk = MOBA_BLOCK
    dh = MOBA_DH
    nblk = seq // blk
    n_sel = min(MOBA_TOPK, nblk - 1)
    pad_rows = 2 * SUBLANES

    vaug_ref[:, :dh] = v_ref[...]
    vaug_ref[:, dh:] = jnp.ones((seq, dh), vaug_ref.dtype)
    kaug_ref[:, :dh] = k_ref[...]
    key_blk = lax.broadcasted_iota(jnp.int32, (seq, dh), 0) // blk
    key_lane = lax.broadcasted_iota(jnp.int32, (seq, dh), 1)
    kaug_ref[:, dh:] = jnp.where(key_blk == key_lane, 1.0, 0.0).astype(kaug_ref.dtype)

    blk_row = lax.broadcasted_iota(jnp.int32, (pad_rows, seq), 0)
    blk_col = lax.broadcasted_iota(jnp.int32, (pad_rows, seq), 1) // blk
    member = jnp.where(blk_row == blk_col, 1.0, 0.0).astype(BF16)
    k_mean = jnp.dot(member, k_ref[...], preferred_element_type=F32) * (1.0 / blk)
    k_mean_hi = k_mean.astype(BF16)
    k_mean_lo = (k_mean - k_mean_hi.astype(F32)).astype(BF16)
    q_all = q_ref[...]
    gate_t = (lax.dot_general(k_mean_hi, q_all, _NT, preferred_element_type=F32)
              + lax.dot_general(k_mean_lo, q_all, _NT, preferred_element_type=F32))

    row = lax.broadcasted_iota(jnp.int32, (blk, LANE), 0)
    lane = lax.broadcasted_iota(jnp.int32, (blk, LANE), 1)
    blk_id = lax.broadcasted_iota(jnp.int32, (SUBLANES, blk), 0)

    def scores(i):
        q = q_ref[i * blk:(i + 1) * blk, :]
        nk = (i + 1) * blk
        if i > n_sel:
            g = gate_t[:SUBLANES, i * blk:(i + 1) * blk]
            bias = jnp.zeros((SUBLANES, blk), F32)
            for j in range(i):
                gj = g[j:j + 1, :]
                beats = ((g > gj) | ((g == gj) & (blk_id < j))) & (blk_id < i)
                rank = jnp.sum(jnp.where(beats, 1.0, 0.0), axis=0, keepdims=True)
                bias = jnp.where((blk_id == j) & (rank >= float(n_sel)), MOBA_MASK, bias)
            bias_t = jnp.concatenate([bias, jnp.zeros((LANE - SUBLANES, blk), F32)], axis=0)
            q_aug = jnp.concatenate([q, bias_t.T.astype(BF16)], axis=1)
            s = lax.dot_general(q_aug, kaug_ref[0:nk, :], _NT, preferred_element_type=F32)
        else:
            s = lax.dot_general(q, k_ref[0:nk, :], _NT, preferred_element_type=F32)
        chunks = [s[:, c * LANE:(c + 1) * LANE] for c in range(nk // LANE)]
        for t, c in enumerate(range(i * blk // LANE, nk // LANE)):
            chunks[c] = jnp.where(lane + t * LANE <= row, chunks[c], float("-inf"))
        return chunks

    def attend(i, chunks):
        nk = (i + 1) * blk
        m_el = chunks[0]
        for ch in chunks[1:]:
            m_el = jnp.maximum(m_el, ch)
        m = m_el.max(axis=-1, keepdims=True)
        p = jnp.concatenate([jnp.exp(ch - m).astype(BF16) for ch in chunks], axis=1)
        o_aug = jnp.dot(p, vaug_ref[0:nk, :], preferred_element_type=F32)
        o_ref[i * blk:(i + 1) * blk, :] = (o_aug[:, :dh] / o_aug[:, dh:]).astype(o_ref.dtype)

    pending = scores(0)
    for i in range(nblk):
        current = pending
        if i + 1 < nblk:
            pending = scores(i + 1)
        attend(i, current)


def moba_core(q, k, v, batch, seq):
    m, d = q.shape
    assert seq % MOBA_BLOCK == 0 and MOBA_DH == LANE and seq // MOBA_BLOCK <= SUBLANES
    blk = pl.BlockSpec((seq, MOBA_DH), lambda b, h: (b, h))
    return pl.pallas_call(
        functools.partial(_moba_kernel, seq=seq),
        out_shape=jax.ShapeDtypeStruct((m, d), BF16),
        grid=(batch, d // MOBA_DH),
        in_specs=[blk, blk, blk],
        out_specs=blk,
        scratch_shapes=[pltpu.VMEM((seq, 2 * MOBA_DH), BF16),
                        pltpu.VMEM((seq, 2 * MOBA_DH), BF16)],
        compiler_params=_params(2),
        name="moba_core",
    )(q, k, v)


def moba_mixer(xn, cosf, sins, w_in, w_out, j, batch, seq):
    d = D_MODEL
    rot = lambda scale: functools.partial(_ep_rotary_roll, head_dim=MOBA_DH, scale=scale)
    q = matmul(xn, w_in, (j,), col_offsets=(0,), n_cols=d, epilogue=rot(MOBA_DH ** -0.5),
               out_dtype=BF16, row_extras=(cosf, sins), name="moba_q")
    k = matmul(xn, w_in, (j,), col_offsets=(d,), n_cols=d, epilogue=rot(1.0),
               out_dtype=BF16, row_extras=(cosf, sins), name="moba_k")
    v = matmul(xn, w_in, (j,), col_offsets=(2 * d,), n_cols=d, epilogue=_ep_plain,
               out_dtype=BF16, name="moba_v")
    o = moba_core(q, k, v, batch, seq)
    return matmul(o, w_out, (j,), col_offsets=(0,), n_cols=d, epilogue=_ep_plain,
                  out_dtype=BF16, name="moba_out")


def kernel(x, positions, norm_g, ffn_w_in, ffn_w_out, ret_w_in, ret_gn_g, ret_w_out,
           sg_w_in, sg_ln_g, sg_w_s, sg_b, sg_w_out, moba_w_in, moba_w_out):
    batch, seq, d = x.shape
    m = batch * seq
    pos = positions.reshape(m, 1).astype(F32)

    ret_inv = 1.0 / (10000.0 ** jnp.linspace(0.0, 1.0, RET_DK // 2, dtype=F32))
    ret_ang = pos * ret_inv
    ret_cos, ret_sin = jnp.cos(ret_ang), jnp.sin(ret_ang)
    moba_inv = 1.0 / (ROPE_THETA ** (jnp.arange(0, MOBA_DH, 2, dtype=F32) / MOBA_DH))
    moba_ang = pos * moba_inv
    mc, ms = jnp.cos(moba_ang), jnp.sin(moba_ang)
    moba_cosf = jnp.concatenate([mc, mc], axis=1)
    moba_sins = jnp.concatenate([-ms, ms], axis=1)

    h = x.reshape(m, d)
    xn = pre_norm(h, norm_g[0, 0])
    for i in range(DEPTH):
        mix, j = i % N_MIXERS, i // N_MIXERS
        g = norm_g[i]
        y = ffn(xn, ffn_w_in, ffn_w_out, (i, 0))
        h, xn = post_norm(y, h, g[1], g[2], 0.5)
        if mix == 0:
            y = retention_mixer(xn, ret_cos, ret_sin, ret_w_in, ret_gn_g, ret_w_out,
                                j, batch, seq)
        elif mix == 1:
            y = spatial_gating_mixer(xn, sg_w_in, sg_ln_g, sg_w_s, sg_b, sg_w_out, j)
        else:
            y = moba_mixer(xn, moba_cosf, moba_sins, moba_w_in, moba_w_out, j, batch, seq)
        h, xn = post_norm(y, h, g[3], g[4], 1.0)
        y = ffn(xn, ffn_w_in, ffn_w_out, (i, 1))
        g_next = norm_g[i + 1, 0] if i + 1 < DEPTH else None
        h, xn = post_norm(y, h, g[5], g_next, 0.5)
    return h.reshape(batch, seq, d)
```

```python
import functools
import math

import jax
import jax.numpy as jnp
from jax import lax
from jax.experimental import pallas as pl
from jax.experimental.pallas import tpu as pltpu

F32 = jnp.float32
BF16 = jnp.bfloat16

D_MODEL = 4096
D_FF = 5632
DEPTH = 4
N_MIXERS = 3
RET_HEADS = 16
RET_DK = D_MODEL // RET_HEADS
RET_CHUNK = 128
RET_UNROLL = 16
SG_GROUPS = 16
SG_CHUNK = 128
SG_ROWS = 512
SG_DG = D_MODEL // SG_GROUPS
MOBA_HEADS = 32
MOBA_DH = D_MODEL // MOBA_HEADS
MOBA_BLOCK = 256
MOBA_TOPK = 3
ROPE_THETA = 10000.0
EPS = 1e-6

LANE = 128
VMEM_LIMIT = 56 * 1024 * 1024
ROW_TILE = 256
MM_TM = 1024
MM_W_BLOCK_BYTES = 8 * 1024 * 1024


def _params(n_grid):
    return pltpu.CompilerParams(
        dimension_semantics=("arbitrary",) * n_grid, vmem_limit_bytes=VMEM_LIMIT)


def _rms(x, g):
    return x * lax.rsqrt(jnp.mean(x * x, axis=-1, keepdims=True) + EPS) * g


def _pre_norm_kernel(h_ref, g_ref, xn_ref):
    xn_ref[...] = _rms(h_ref[...], g_ref[...]).astype(BF16)


def pre_norm(h, g):
    m, d = h.shape
    return pl.pallas_call(
        _pre_norm_kernel,
        out_shape=jax.ShapeDtypeStruct((m, d), BF16),
        grid=(m // ROW_TILE,),
        in_specs=[pl.BlockSpec((ROW_TILE, d), lambda i: (i, 0)),
                  pl.BlockSpec((1, d), lambda i: (0, 0))],
        out_specs=pl.BlockSpec((ROW_TILE, d), lambda i: (i, 0)),
        compiler_params=_params(1),
        name="pre_norm",
    )(h, g.reshape(1, d))


def _post_norm_kernel(y_ref, h_ref, gpost_ref, gpre_ref, hout_ref, xn_ref, *, coef):
    h = h_ref[...] + coef * _rms(y_ref[...].astype(F32), gpost_ref[...])
    hout_ref[...] = h
    xn_ref[...] = _rms(h, gpre_ref[...]).astype(BF16)


def _post_norm_last_kernel(y_ref, h_ref, gpost_ref, hout_ref, *, coef):
    hout_ref[...] = h_ref[...] + coef * _rms(y_ref[...].astype(F32), gpost_ref[...])


def post_norm(y, h, g_post, g_pre, coef):
    m, d = h.shape
    row = pl.BlockSpec((ROW_TILE, d), lambda i: (i, 0))
    vec = pl.BlockSpec((1, d), lambda i: (0, 0))
    if g_pre is None:
        return pl.pallas_call(
            functools.partial(_post_norm_last_kernel, coef=coef),
            out_shape=jax.ShapeDtypeStruct((m, d), F32),
            grid=(m // ROW_TILE,),
            in_specs=[row, row, vec],
            out_specs=row,
            compiler_params=_params(1),
            name="post_norm_last",
        )(y, h, g_post.reshape(1, d)), None
    return pl.pallas_call(
        functools.partial(_post_norm_kernel, coef=coef),
        out_shape=(jax.ShapeDtypeStruct((m, d), F32), jax.ShapeDtypeStruct((m, d), BF16)),
        grid=(m // ROW_TILE,),
        in_specs=[row, row, vec, vec],
        out_specs=(row, row),
        compiler_params=_params(1),
        name="post_norm",
    )(y, h, g_post.reshape(1, d), g_pre.reshape(1, d))


def _mm_kernel(*refs, n_parts, n_extra, epilogue, cast_once):
    x_ref = refs[0]
    w_refs = refs[1:1 + n_parts]
    extra_refs = refs[1 + n_parts:1 + n_parts + n_extra]
    out_ref = refs[1 + n_parts + n_extra]
    if cast_once:
        wbf_refs = refs[2 + n_parts + n_extra:]

        @pl.when(pl.program_id(1) == 0)
        def _():
            for w_ref, wbf_ref in zip(w_refs, wbf_refs):
                wbf_ref[...] = w_ref[...].astype(BF16)

        ws = [wbf_ref[...] for wbf_ref in wbf_refs]
    else:
        ws = [w_ref[...].astype(BF16) for w_ref in w_refs]
    x = x_ref[...]
    accs = [jnp.dot(x, w, preferred_element_type=F32) for w in ws]
    out_ref[...] = epilogue(accs, extra_refs).astype(out_ref.dtype)


def _mm_col_tile(k, n_parts):
    tn = LANE
    while n_parts * k * (2 * tn) * 4 <= MM_W_BLOCK_BYTES:
        tn *= 2
    return tn


def matmul(x, w, lead, *, col_offsets, n_cols, epilogue, out_dtype, row_extras=(), name,
           weight_stationary=False, cast_once=False):
    m, k = x.shape
    tm = MM_TM
    n_parts = len(col_offsets)
    tn = _mm_col_tile(k, n_parts)
    assert m % tm == 0 and n_cols % tn == 0 and all(o % tn == 0 for o in col_offsets)
    assert w.shape[len(lead)] == k and (weight_stationary or not cast_once)
    if weight_stationary:
        grid = (n_cols // tn, m // tm)
        ij = lambda f: (lambda j, i: f(i, j))
    else:
        grid = (m // tm, n_cols // tn)
        ij = lambda f: f
    in_specs = [pl.BlockSpec((tm, k), ij(lambda i, j: (i, 0)))]
    w_block = (None,) * len(lead) + (k, tn)
    for off in col_offsets:
        in_specs.append(pl.BlockSpec(w_block, ij(lambda i, j, ob=off // tn: (*lead, 0, ob + j))))
    for e in row_extras:
        in_specs.append(pl.BlockSpec((tm, e.shape[1]), ij(lambda i, j: (i, 0))))
    kern = functools.partial(_mm_kernel, n_parts=n_parts, n_extra=len(row_extras),
                             epilogue=epilogue, cast_once=cast_once)
    return pl.pallas_call(
        kern,
        out_shape=jax.ShapeDtypeStruct((m, n_cols), out_dtype),
        grid=grid,
        in_specs=in_specs,
        out_specs=pl.BlockSpec((tm, tn), ij(lambda i, j: (i, j))),
        scratch_shapes=[pltpu.VMEM((k, tn), BF16)] * (n_parts if cast_once else 0),
        compiler_params=_params(2),
        name=name,
    )(x, *([w] * n_parts), *row_extras)


def _ep_plain(accs, extras):
    return accs[0]


def _ep_swiglu(accs, extras):
    a, b = accs
    return a * jax.nn.sigmoid(a) * b


def _ep_silu(accs, extras):
    a = accs[0]
    return a * jax.nn.sigmoid(a)


def _ep_gelu(accs, extras):
    a = accs[0]
    return 0.5 * a * (1.0 + lax.erf(a * math.sqrt(0.5)))


def _ep_rotary_split(accs, extras, *, head_dim, scale):
    acc = accs[0]
    cos = extras[0][...]
    sin = extras[1][...]
    half = head_dim // 2
    outs = []
    for h0 in range(0, acc.shape[1], head_dim):
        t1 = acc[:, h0:h0 + half]
        t2 = acc[:, h0 + half:h0 + head_dim]
        outs.append(t1 * cos - t2 * sin)
        outs.append(t1 * sin + t2 * cos)
    out = jnp.concatenate(outs, axis=1)
    return out * scale if scale != 1.0 else out


def _ep_rotary_roll(accs, extras, *, head_dim, scale):
    acc = accs[0]
    cosf = extras[0][...]
    sins = extras[1][...]
    outs = []
    for h0 in range(0, acc.shape[1], head_dim):
        t = acc[:, h0:h0 + head_dim]
        outs.append(t * cosf + pltpu.roll(t, head_dim // 2, axis=1) * sins)
    out = jnp.concatenate(outs, axis=1)
    return out * scale if scale != 1.0 else out


def ffn(xn, w_in, w_out, lead):
    hid = matmul(xn, w_in, lead, col_offsets=(0, D_FF), n_cols=D_FF,
                 epilogue=_ep_swiglu, out_dtype=BF16, name="ffn_in", weight_stationary=True)
    return matmul(hid, w_out, lead, col_offsets=(0,), n_cols=D_MODEL,
                  epilogue=_ep_plain, out_dtype=BF16, name="ffn_out")


_NT = (((1,), (1,)), ((), ()))
_TN = (((0,), (0,)), ((), ()))


def _retention_kernel(lg_ref, q_ref, k_ref, v_ref, sg_ref, gn_ref, o_ref, state_ref, *, seq):
    c = RET_CHUNK
    dk = RET_DK
    lg = lg_ref[0]
    row = lax.broadcasted_iota(jnp.int32, (c, c), 0)
    col = lax.broadcasted_iota(jnp.int32, (c, c), 1)
    dist = (row - col).astype(F32)
    intra_decay = jnp.where(dist >= 0, jnp.exp(lg[:, :c] * jnp.maximum(dist, 0.0)), 0.0)
    idx = lax.broadcasted_iota(jnp.int32, (c, dk), 0).astype(F32)
    q_decay = jnp.exp(lg * (idx + 1.0))
    k_decay = jnp.exp(lg * (c - 1.0 - idx))
    chunk_decay = jnp.exp(lg * float(c))
    gn = gn_ref[...]
    state_ref[...] = jnp.zeros_like(state_ref)

    def front(rows):
        qc = q_ref[rows, :]
        kc = k_ref[rows, :]
        vc = v_ref[rows, :]
        scores = lax.dot_general(qc, kc, _NT, preferred_element_type=F32) * intra_decay
        intra = jnp.dot(scores.astype(BF16), vc, preferred_element_type=F32)
        qd = (qc.astype(F32) * q_decay).astype(BF16)
        kd = (kc.astype(F32) * k_decay).astype(BF16)
        update = lax.dot_general(kd, vc, _TN, preferred_element_type=F32)
        return qd, intra, update

    def back(rows, qd, intra, update):
        state = state_ref[...]
        cross = jnp.dot(qd, state.astype(BF16), preferred_element_type=F32)
        state_ref[...] = state * chunk_decay + update
        o = intra + cross
        mu = jnp.mean(o, axis=-1, keepdims=True)
        var = jnp.mean(jnp.square(o - mu), axis=-1, keepdims=True)
        on = (o - mu) * lax.rsqrt(var + EPS) * gn
        o_ref[rows, :] = (sg_ref[rows, :].astype(F32) * on).astype(o_ref.dtype)

    def group(gi, carry):
        rows = lambda u: pl.ds(pl.multiple_of((gi * RET_UNROLL + u) * c, c), c)
        pending = front(rows(0))
        for u in range(RET_UNROLL):
            current = pending
            if u + 1 < RET_UNROLL:
                pending = front(rows(u + 1))
            back(rows(u), *current)
        return carry

    lax.fori_loop(0, seq // (c * RET_UNROLL), group, 0)


def retention_core(q, k, v, sg, gn_g, batch, seq):
    m, d = q.shape
    log_gamma = jnp.log1p(-jnp.exp2(-5.0 - jnp.arange(RET_HEADS, dtype=F32)))
    lg = jnp.broadcast_to(log_gamma[:, None, None], (RET_HEADS, 1, RET_DK))
    blk = pl.BlockSpec((seq, RET_DK), lambda b, h: (b, h))
    return pl.pallas_call(
        functools.partial(_retention_kernel, seq=seq),
        out_shape=jax.ShapeDtypeStruct((m, d), BF16),
        grid=(batch, RET_HEADS),
        in_specs=[pl.BlockSpec((1, 1, RET_DK), lambda b, h: (h, 0, 0)),
                  blk, blk, blk, blk,
                  pl.BlockSpec((1, RET_DK), lambda b, h: (0, h))],
        out_specs=blk,
        scratch_shapes=[pltpu.VMEM((RET_DK, RET_DK), F32)],
        compiler_params=_params(2),
        name="retention_core",
    )(lg, q, k, v, sg, gn_g.reshape(1, d))


def retention_mixer(xn, cos, sin, w_in, gn_g, w_out, j, batch, seq):
    d = D_MODEL
    rot = lambda scale: functools.partial(_ep_rotary_split, head_dim=RET_DK, scale=scale)
    q = matmul(xn, w_in, (j,), col_offsets=(0,), n_cols=d, epilogue=rot(1.0),
               out_dtype=BF16, row_extras=(cos, sin), name="ret_q", weight_stationary=True)
    k = matmul(xn, w_in, (j,), col_offsets=(d,), n_cols=d, epilogue=rot(RET_DK ** -0.5),
               out_dtype=BF16, row_extras=(cos, sin), name="ret_k", weight_stationary=True)
    v = matmul(xn, w_in, (j,), col_offsets=(2 * d,), n_cols=d, epilogue=_ep_plain,
               out_dtype=BF16, name="ret_v", weight_stationary=True)
    sg = matmul(xn, w_in, (j,), col_offsets=(3 * d,), n_cols=d, epilogue=_ep_silu,
                out_dtype=BF16, name="ret_g", weight_stationary=True)
    o = retention_core(q, k, v, sg, gn_g[j], batch, seq)
    return matmul(o, w_out, (j,), col_offsets=(0,), n_cols=d, epilogue=_ep_plain,
                  out_dtype=BF16, name="ret_out", weight_stationary=True)


def _sgu_kernel(u_ref, v_ref, lng_ref, ws_ref, bs_ref, o_ref, wm_ref):
    c = SG_CHUNK
    row = lax.broadcasted_iota(jnp.int32, (c, c), 0)
    col = lax.broadcasted_iota(jnp.int32, (c, c), 1)
    causal = row >= col
    for g in range(SG_GROUPS):
        wm_ref[g] = jnp.where(causal, ws_ref[g], 0.0).astype(wm_ref.dtype)
    bs = bs_ref[...]
    lng = lng_ref[...]

    def chunk(n, carry):
        rows = pl.ds(pl.multiple_of(n * c, c), c)
        v = v_ref[rows, :].astype(F32)
        mu = jnp.mean(v, axis=-1, keepdims=True)
        var = jnp.mean(jnp.square(v - mu), axis=-1, keepdims=True)
        vn = ((v - mu) * lax.rsqrt(var + EPS) * lng).astype(BF16)
        for g in range(SG_GROUPS):
            cols = slice(g * SG_DG, (g + 1) * SG_DG)
            mixed = jnp.dot(wm_ref[g], vn[:, cols], preferred_element_type=F32) + bs[:, g:g + 1]
            o_ref[rows, cols] = (u_ref[rows, cols].astype(F32) * mixed).astype(o_ref.dtype)
        return carry

    lax.fori_loop(0, SG_ROWS // c, chunk, 0)


def sgu_core(uv, ln_g, w_s, b_s):
    m = uv.shape[0]
    d = D_MODEL
    c = SG_CHUNK
    assert m % SG_ROWS == 0 and SG_ROWS % c == 0
    return pl.pallas_call(
        _sgu_kernel,
        out_shape=jax.ShapeDtypeStruct((m, d), BF16),
        grid=(m // SG_ROWS,),
        in_specs=[pl.BlockSpec((SG_ROWS, d), lambda n: (n, 0)),
                  pl.BlockSpec((SG_ROWS, d), lambda n: (n, 1)),
                  pl.BlockSpec((1, d), lambda n: (0, 0)),
                  pl.BlockSpec((SG_GROUPS, c, c), lambda n: (0, 0, 0)),
                  pl.BlockSpec((c, SG_GROUPS), lambda n: (0, 0))],
        out_specs=pl.BlockSpec((SG_ROWS, d), lambda n: (n, 0)),
        scratch_shapes=[pltpu.VMEM((SG_GROUPS, c, c), BF16)],
        compiler_params=_params(1),
        name="sgu_core",
    )(uv, uv, ln_g.reshape(1, d), w_s, b_s.T)


def spatial_gating_mixer(xn, w_in, ln_g, w_s, b_s, w_out, j):
    d = D_MODEL
    uv = matmul(xn, w_in, (j,), col_offsets=(0,), n_cols=2 * d, epilogue=_ep_gelu,
                out_dtype=BF16, name="sg_in", weight_stationary=True, cast_once=True)
    o = sgu_core(uv, ln_g[j], w_s[j], b_s[j])
    return matmul(o, w_out, (j,), col_offsets=(0,), n_cols=d, epilogue=_ep_plain,
                  out_dtype=BF16, name="sg_out", weight_stationary=True, cast_once=True)


MOBA_MASK = -1e30
SUBLANES = 8


def _moba_kernel(q_ref, k_ref, v_ref, o_ref, kaug_ref, vaug_ref, *, seq):
    blk = MOBA_BLOCK
    dh = MOBA_DH
    nblk = seq // blk
    n_sel = min(MOBA_TOPK, nblk - 1)
    pad_rows = 2 * SUBLANES

    vaug_ref[:, :dh] = v_ref[...]
    vaug_ref[:, dh:] = jnp.ones((seq, dh), vaug_ref.dtype)
    kaug_ref[:, :dh] = k_ref[...]
    key_blk = lax.broadcasted_iota(jnp.int32, (seq, dh), 0) // blk
    key_lane = lax.broadcasted_iota(jnp.int32, (seq, dh), 1)
    kaug_ref[:, dh:] = jnp.where(key_blk == key_lane, 1.0, 0.0).astype(kaug_ref.dtype)

    blk_row = lax.broadcasted_iota(jnp.int32, (pad_rows, seq), 0)
    blk_col = lax.broadcasted_iota(jnp.int32, (pad_rows, seq), 1) // blk
    member = jnp.where(blk_row == blk_col, 1.0, 0.0).astype(BF16)
    k_mean = jnp.dot(member, k_ref[...], preferred_element_type=F32) * (1.0 / blk)
    k_mean_hi = k_mean.astype(BF16)
    k_mean_lo = (k_mean - k_mean_hi.astype(F32)).astype(BF16)
    q_all = q_ref[...]
    gate_t = (lax.dot_general(k_mean_hi, q_all, _NT, preferred_element_type=F32)
              + lax.dot_general(k_mean_lo, q_all, _NT, preferred_element_type=F32))

    row = lax.broadcasted_iota(jnp.int32, (blk, LANE), 0)
    lane = lax.broadcasted_iota(jnp.int32, (blk, LANE), 1)
    blk_id = lax.broadcasted_iota(jnp.int32, (SUBLANES, blk), 0)

    def scores(i):
        q = q_ref[i * blk:(i + 1) * blk, :]
        nk = (i + 1) * blk
        if i > n_sel:
            g = gate_t[:SUBLANES, i * blk:(i + 1) * blk]
            bias = jnp.zeros((SUBLANES, blk), F32)
            for j in range(i):
                gj = g[j:j + 1, :]
                beats = ((g > gj) | ((g == gj) & (blk_id < j))) & (blk_id < i)
                rank = jnp.sum(jnp.where(beats, 1.0, 0.0), axis=0, keepdims=True)
                bias = jnp.where((blk_id == j) & (rank >= float(n_sel)), MOBA_MASK, bias)
            bias_t = jnp.concatenate([bias, jnp.zeros((LANE - SUBLANES, blk), F32)], axis=0)
            q_aug = jnp.concatenate([q, bias_t.T.astype(BF16)], axis=1)
            s = lax.dot_general(q_aug, kaug_ref[0:nk, :], _NT, preferred_element_type=F32)
        else:
            s = lax.dot_general(q, k_ref[0:nk, :], _NT, preferred_element_type=F32)
        chunks = [s[:, c * LANE:(c + 1) * LANE] for c in range(nk // LANE)]
        for t, c in enumerate(range(i * blk // LANE, nk // LANE)):
            chunks[c] = jnp.where(lane + t * LANE <= row, chunks[c], float("-inf"))
        return chunks

    def attend(i, chunks):
        nk = (i + 1) * blk
        m_el = chunks[0]
        for ch in chunks[1:]:
            m_el = jnp.maximum(m_el, ch)
        m = m_el.max(axis=-1, keepdims=True)
        p = jnp.concatenate([jnp.exp(ch - m).astype(BF16) for ch in chunks], axis=1)
        o_aug = jnp.dot(p, vaug_ref[0:nk, :], preferred_element_type=F32)
        o_ref[i * blk:(i + 1) * blk, :] = (o_aug[:, :dh] / o_aug[:, dh:]).astype(o_ref.dtype)

    pending = scores(0)
    for i in range(nblk):
        current = pending
        if i + 1 < nblk:
            pending = scores(i + 1)
        attend(i, current)


def moba_core(q, k, v, batch, seq):
    m, d = q.shape
    assert seq % MOBA_BLOCK == 0 and MOBA_DH == LANE and seq // MOBA_BLOCK <= SUBLANES
    blk = pl.BlockSpec((seq, MOBA_DH), lambda b, h: (b, h))
    return pl.pallas_call(
        functools.partial(_moba_kernel, seq=seq),
        out_shape=jax.ShapeDtypeStruct((m, d), BF16),
        grid=(batch, d // MOBA_DH),
        in_specs=[blk, blk, blk],
        out_specs=blk,
        scratch_shapes=[pltpu.VMEM((seq, 2 * MOBA_DH), BF16),
                        pltpu.VMEM((seq, 2 * MOBA_DH), BF16)],
        compiler_params=_params(2),
        name="moba_core",
    )(q, k, v)


def moba_mixer(xn, cosf, sins, w_in, w_out, j, batch, seq):
    d = D_MODEL
    rot = lambda scale: functools.partial(_ep_rotary_roll, head_dim=MOBA_DH, scale=scale)
    q = matmul(xn, w_in, (j,), col_offsets=(0,), n_cols=d, epilogue=rot(MOBA_DH ** -0.5),
               out_dtype=BF16, row_extras=(cosf, sins), name="moba_q", weight_stationary=True, cast_once=True)
    k = matmul(xn, w_in, (j,), col_offsets=(d,), n_cols=d, epilogue=rot(1.0),
               out_dtype=BF16, row_extras=(cosf, sins), name="moba_k", weight_stationary=True, cast_once=True)
    v = matmul(xn, w_in, (j,), col_offsets=(2 * d,), n_cols=d, epilogue=_ep_plain,
               out_dtype=BF16, name="moba_v", weight_stationary=True, cast_once=True)
    o = moba_core(q, k, v, batch, seq)
    return matmul(o, w_out, (j,), col_offsets=(0,), n_cols=d, epilogue=_ep_plain,
                  out_dtype=BF16, name="moba_out", weight_stationary=True, cast_once=True)


def kernel(x, positions, norm_g, ffn_w_in, ffn_w_out, ret_w_in, ret_gn_g, ret_w_out,
           sg_w_in, sg_ln_g, sg_w_s, sg_b, sg_w_out, moba_w_in, moba_w_out):
    batch, seq, d = x.shape
    m = batch * seq
    pos = positions.reshape(m, 1).astype(F32)

    ret_inv = 1.0 / (10000.0 ** jnp.linspace(0.0, 1.0, RET_DK // 2, dtype=F32))
    ret_ang = pos * ret_inv
    ret_cos, ret_sin = jnp.cos(ret_ang), jnp.sin(ret_ang)
    moba_inv = 1.0 / (ROPE_THETA ** (jnp.arange(0, MOBA_DH, 2, dtype=F32) / MOBA_DH))
    moba_ang = pos * moba_inv
    mc, ms = jnp.cos(moba_ang), jnp.sin(moba_ang)
    moba_cosf = jnp.concatenate([mc, mc], axis=1)
    moba_sins = jnp.concatenate([-ms, ms], axis=1)

    h = x.reshape(m, d)
    xn = pre_norm(h, norm_g[0, 0])
    for i in range(DEPTH):
        mix, j = i % N_MIXERS, i // N_MIXERS
        g = norm_g[i]
        y = ffn(xn, ffn_w_in, ffn_w_out, (i, 0))
        h, xn = post_norm(y, h, g[1], g[2], 0.5)
        if mix == 0:
            y = retention_mixer(xn, ret_cos, ret_sin, ret_w_in, ret_gn_g, ret_w_out,
                                j, batch, seq)
        elif mix == 1:
            y = spatial_gating_mixer(xn, sg_w_in, sg_ln_g, sg_w_s, sg_b, sg_w_out, j)
        else:
            y = moba_mixer(xn, moba_cosf, moba_sins, moba_w_in, moba_w_out, j, batch, seq)
        h, xn = post_norm(y, h, g[3], g[4], 1.0)
        y = ffn(xn, ffn_w_in, ffn_w_out, (i, 1))
        g_next = norm_g[i + 1, 0] if i + 1 < DEPTH else None
        h, xn = post_norm(y, h, g[5], g_next, 0.5)
    return h.reshape(batch, seq, d)
```

```python
import functools
import math

import jax
import jax.numpy as jnp
from jax import lax
from jax.experimental import pallas as pl
from jax.experimental.pallas import tpu as pltpu

F32 = jnp.float32
BF16 = jnp.bfloat16

D_MODEL = 4096
D_FF = 5632
DEPTH = 4
N_MIXERS = 3
RET_HEADS = 16
RET_DK = D_MODEL // RET_HEADS
RET_CHUNK = 128
RET_UNROLL = 16
SG_GROUPS = 16
SG_CHUNK = 128
SG_ROWS = 512
SG_DG = D_MODEL // SG_GROUPS
MOBA_HEADS = 32
MOBA_DH = D_MODEL // MOBA_HEADS
MOBA_BLOCK = 256
MOBA_TOPK = 3
ROPE_THETA = 10000.0
EPS = 1e-6

LANE = 128
SUBLANES = 8
VMEM_LIMIT = 56 * 1024 * 1024
ROW_TILE = 256
PN_GROUPS = 4
MM_TM = 1024
MM_W_BLOCK_BYTES = 8 * 1024 * 1024


def _params(n_grid):
    return pltpu.CompilerParams(
        dimension_semantics=("arbitrary",) * n_grid, vmem_limit_bytes=VMEM_LIMIT)


def _rms(x, g):
    return x * lax.rsqrt(jnp.mean(x * x, axis=-1, keepdims=True) + EPS) * g


def _pre_norm_kernel(h_ref, g_ref, xn_ref):
    xn_ref[...] = _rms(h_ref[...], g_ref[...]).astype(BF16)


def _gain_spec(idx):
    return pl.BlockSpec((None, 1, D_MODEL), lambda *_: (idx, 0, 0))


def pre_norm(h, gains, g_idx):
    m, d = h.shape
    return pl.pallas_call(
        _pre_norm_kernel,
        out_shape=jax.ShapeDtypeStruct((m, d), BF16),
        grid=(m // ROW_TILE,),
        in_specs=[pl.BlockSpec((ROW_TILE, d), lambda i: (i, 0)), _gain_spec(g_idx)],
        out_specs=pl.BlockSpec((ROW_TILE, d), lambda i: (i, 0)),
        compiler_params=_params(1),
        name="pre_norm",
    )(h, gains)


def _post_norm_rows(y_ref, h_ref, gpost_ref, gpre_ref, hout_ref, xn_ref, coef):
    h = h_ref[...] + coef * _rms(y_ref[...].astype(F32), gpost_ref[...])
    hout_ref[...] = h
    if xn_ref is not None:
        xn_ref[...] = _rms(h, gpre_ref[...]).astype(BF16)


def _split_norm_refs(refs, has_pre, n_carry):
    gpost_ref = refs[0]
    gpre_ref = refs[1] if has_pre else None
    return gpost_ref, gpre_ref, refs[1 + has_pre + n_carry:]


def _post_norm_kernel(y_ref, h_ref, *refs, coef, has_pre, n_carry):
    gpost_ref, gpre_ref, outs = _split_norm_refs(refs, has_pre, n_carry)
    _post_norm_rows(y_ref, h_ref, gpost_ref, gpre_ref, outs[0], outs[1] if has_pre else None, coef)


def _mm_kernel(*refs, n_parts, n_extra, epilogue):
    x_ref = refs[0]
    w_refs = refs[1:1 + n_parts]
    extra_refs = refs[1 + n_parts:1 + n_parts + n_extra]
    out_ref = refs[1 + n_parts + n_extra]
    x = x_ref[...]
    accs = [jnp.dot(x, w_ref[...].astype(BF16), preferred_element_type=F32) for w_ref in w_refs]
    out_ref[...] = epilogue(accs, extra_refs).astype(out_ref.dtype)


def _mm_col_tile(k, n_parts):
    tn = LANE
    while n_parts * k * (2 * tn) * 4 <= MM_W_BLOCK_BYTES:
        tn *= 2
    return tn


def matmul(x, w, lead, *, col_offsets, n_cols, epilogue, out_dtype, row_extras=(), name,
           row_tiles=None):
    m, k = x.shape
    tm = MM_TM
    tn = _mm_col_tile(k, len(col_offsets))
    assert m % tm == 0 and n_cols % tn == 0 and all(o % tn == 0 for o in col_offsets)
    assert w.shape[len(lead)] == k
    row0, n_row = (0, m // tm) if row_tiles is None else row_tiles
    in_specs = [pl.BlockSpec((tm, k), lambda i, j: (row0 + i, 0))]
    w_block = (None,) * len(lead) + (k, tn)
    for off in col_offsets:
        in_specs.append(pl.BlockSpec(w_block, lambda i, j, ob=off // tn: (*lead, 0, ob + j)))
    for e in row_extras:
        in_specs.append(pl.BlockSpec((tm, e.shape[1]), lambda i, j: (row0 + i, 0)))
    kern = functools.partial(_mm_kernel, n_parts=len(col_offsets), n_extra=len(row_extras),
                             epilogue=epilogue)
    return pl.pallas_call(
        kern,
        out_shape=jax.ShapeDtypeStruct((n_row * tm, n_cols), out_dtype),
        grid=(n_row, n_cols // tn),
        in_specs=in_specs,
        out_specs=pl.BlockSpec((tm, tn), lambda i, j: (i, j)),
        compiler_params=_params(2),
        name=name,
    )(x, *([w] * len(col_offsets)), *row_extras)


def _mm_post_kernel(x_ref, w_ref, yprev_ref, h_ref, *refs, coef, has_pre, n_carry):
    gpost_ref, gpre_ref, outs = _split_norm_refs(refs, has_pre, n_carry)
    _post_norm_rows(yprev_ref, h_ref, gpost_ref, gpre_ref, outs[1], outs[2] if has_pre else None, coef)
    y_ref = outs[0]
    y_ref[...] = jnp.dot(x_ref[...], w_ref[...].astype(BF16),
                         preferred_element_type=F32).astype(y_ref.dtype)


def out_proj_post_norm(x, w, lead, h, gains, post_idx, pre_idx, coef, name):
    m, k = x.shape
    d = D_MODEL
    tm = MM_TM
    tn = _mm_col_tile(k, 1)
    n_col = d // tn
    group_rows = m // PN_GROUPS
    tiles = group_rows // tm
    steps = tiles * n_col
    chunk = group_rows // steps
    assert group_rows % tm == 0 and group_rows % steps == 0 and chunk % (2 * SUBLANES) == 0
    assert group_rows % ROW_TILE == 0 and w.shape[len(lead)] == k
    has_pre = pre_idx is not None
    gain_specs = [_gain_spec(post_idx)] + ([_gain_spec(pre_idx)] if has_pre else [])
    gain_args = [gains] * len(gain_specs)
    carry_spec = pl.BlockSpec(memory_space=pl.ANY)
    norm_shapes = [jax.ShapeDtypeStruct((m, d), F32)] + (
        [jax.ShapeDtypeStruct((m, d), BF16)] if has_pre else [])
    w_block = (None,) * len(lead) + (k, tn)

    y_prev = matmul(x, w, lead, col_offsets=(0,), n_cols=d, epilogue=_ep_plain, out_dtype=BF16,
                    name=name, row_tiles=(0, tiles))
    carried = []
    for q in range(1, PN_GROUPS):
        chunk_map = lambda i, j, q=q: ((q - 1) * steps + i * n_col + j, 0)
        in_specs = [pl.BlockSpec((tm, k), lambda i, j, q=q: (q * tiles + i, 0)),
                    pl.BlockSpec(w_block, lambda i, j: (*lead, 0, j)),
                    pl.BlockSpec((chunk, d), lambda i, j: (i * n_col + j, 0)),
                    pl.BlockSpec((chunk, d), chunk_map)] + gain_specs
        outs = pl.pallas_call(
            functools.partial(_mm_post_kernel, coef=coef, has_pre=has_pre, n_carry=len(carried)),
            out_shape=[jax.ShapeDtypeStruct((group_rows, d), BF16)] + norm_shapes,
            grid=(tiles, n_col),
            in_specs=in_specs + [carry_spec] * len(carried),
            out_specs=[pl.BlockSpec((tm, tn), lambda i, j: (i, j))]
            + [pl.BlockSpec((chunk, d), chunk_map)] * len(norm_shapes),
            input_output_aliases={len(in_specs) + c: 1 + c for c in range(len(carried))},
            compiler_params=_params(2),
            name=name + "_post",
        )(x, w, y_prev, h, *gain_args, *carried)
        y_prev, carried = outs[0], list(outs[1:])

    first_row = (PN_GROUPS - 1) * (group_rows // ROW_TILE)
    row_map = lambda i: (first_row + i, 0)
    in_specs = [pl.BlockSpec((ROW_TILE, d), lambda i: (i, 0)),
                pl.BlockSpec((ROW_TILE, d), row_map)] + gain_specs
    outs = pl.pallas_call(
        functools.partial(_post_norm_kernel, coef=coef, has_pre=has_pre, n_carry=len(carried)),
        out_shape=norm_shapes,
        grid=(group_rows // ROW_TILE,),
        in_specs=in_specs + [carry_spec] * len(carried),
        out_specs=[pl.BlockSpec((ROW_TILE, d), row_map)] * len(norm_shapes),
        input_output_aliases={len(in_specs) + c: c for c in range(len(carried))},
        compiler_params=_params(1),
        name="post_norm_tail",
    )(y_prev, h, *gain_args, *carried)
    return outs[0], (outs[1] if has_pre else None)


def _ep_plain(accs, extras):
    return accs[0]


def _ep_swiglu(accs, extras):
    a, b = accs
    return a * jax.nn.sigmoid(a) * b


def _ep_silu(accs, extras):
    a = accs[0]
    return a * jax.nn.sigmoid(a)


def _ep_gelu(accs, extras):
    a = accs[0]
    return 0.5 * a * (1.0 + lax.erf(a * math.sqrt(0.5)))


def _ep_rotary_split(accs, extras, *, head_dim, scale):
    acc = accs[0]
    cos = extras[0][...]
    sin = extras[1][...]
    half = head_dim // 2
    outs = []
    for h0 in range(0, acc.shape[1], head_dim):
        t1 = acc[:, h0:h0 + half]
        t2 = acc[:, h0 + half:h0 + head_dim]
        outs.append(t1 * cos - t2 * sin)
        outs.append(t1 * sin + t2 * cos)
    out = jnp.concatenate(outs, axis=1)
    return out * scale if scale != 1.0 else out


def _ep_rotary_roll(accs, extras, *, head_dim, scale):
    acc = accs[0]
    cosf = extras[0][...]
    sins = extras[1][...]
    outs = []
    for h0 in range(0, acc.shape[1], head_dim):
        t = acc[:, h0:h0 + head_dim]
        outs.append(t * cosf + pltpu.roll(t, head_dim // 2, axis=1) * sins)
    out = jnp.concatenate(outs, axis=1)
    return out * scale if scale != 1.0 else out


def ffn_hidden(xn, w_in, lead):
    return matmul(xn, w_in, lead, col_offsets=(0, D_FF), n_cols=D_FF,
                  epilogue=_ep_swiglu, out_dtype=BF16, name="ffn_in")


_NT = (((1,), (1,)), ((), ()))
_TN = (((0,), (0,)), ((), ()))


def _retention_kernel(lg_ref, q_ref, k_ref, v_ref, sg_ref, gn_ref, o_ref, state_ref, *, seq):
    c = RET_CHUNK
    dk = RET_DK
    lg = lg_ref[0]
    row = lax.broadcasted_iota(jnp.int32, (c, c), 0)
    col = lax.broadcasted_iota(jnp.int32, (c, c), 1)
    dist = (row - col).astype(F32)
    intra_decay = jnp.where(dist >= 0, jnp.exp(lg[:, :c] * jnp.maximum(dist, 0.0)), 0.0)
    idx = lax.broadcasted_iota(jnp.int32, (c, dk), 0).astype(F32)
    q_decay = jnp.exp(lg * (idx + 1.0))
    k_decay = jnp.exp(lg * (c - 1.0 - idx))
    chunk_decay = jnp.exp(lg * float(c))
    gn = gn_ref[...]
    state_ref[...] = jnp.zeros_like(state_ref)

    def front(rows):
        qc = q_ref[rows, :]
        kc = k_ref[rows, :]
        vc = v_ref[rows, :]
        scores = lax.dot_general(qc, kc, _NT, preferred_element_type=F32) * intra_decay
        intra = jnp.dot(scores.astype(BF16), vc, preferred_element_type=F32)
        qd = (qc.astype(F32) * q_decay).astype(BF16)
        kd = (kc.astype(F32) * k_decay).astype(BF16)
        update = lax.dot_general(kd, vc, _TN, preferred_element_type=F32)
        return qd, intra, update

    def back(rows, qd, intra, update):
        state = state_ref[...]
        cross = jnp.dot(qd, state.astype(BF16), preferred_element_type=F32)
        state_ref[...] = state * chunk_decay + update
        o = intra + cross
        mu = jnp.mean(o, axis=-1, keepdims=True)
        var = jnp.mean(jnp.square(o - mu), axis=-1, keepdims=True)
        on = (o - mu) * lax.rsqrt(var + EPS) * gn
        o_ref[rows, :] = (sg_ref[rows, :].astype(F32) * on).astype(o_ref.dtype)

    def group(gi, carry):
        rows = lambda u: pl.ds(pl.multiple_of((gi * RET_UNROLL + u) * c, c), c)
        pending = front(rows(0))
        for u in range(RET_UNROLL):
            current = pending
            if u + 1 < RET_UNROLL:
                pending = front(rows(u + 1))
            back(rows(u), *current)
        return carry

    lax.fori_loop(0, seq // (c * RET_UNROLL), group, 0)


def retention_core(q, k, v, sg, gn_g, j, batch, seq):
    m, d = q.shape
    log_gamma = jnp.log1p(-jnp.exp2(-5.0 - jnp.arange(RET_HEADS, dtype=F32)))
    lg = jnp.broadcast_to(log_gamma[:, None, None], (RET_HEADS, 1, RET_DK))
    blk = pl.BlockSpec((seq, RET_DK), lambda b, h: (b, h))
    return pl.pallas_call(
        functools.partial(_retention_kernel, seq=seq),
        out_shape=jax.ShapeDtypeStruct((m, d), BF16),
        grid=(batch, RET_HEADS),
        in_specs=[pl.BlockSpec((1, 1, RET_DK), lambda b, h: (h, 0, 0)),
                  blk, blk, blk, blk,
                  pl.BlockSpec((None, 1, RET_DK), lambda b, h: (j, 0, h))],
        out_specs=blk,
        scratch_shapes=[pltpu.VMEM((RET_DK, RET_DK), F32)],
        compiler_params=_params(2),
        name="retention_core",
    )(lg, q, k, v, sg, gn_g.reshape(gn_g.shape[0], 1, d))


def retention_mixer(xn, cos, sin, w_in, gn_g, j, batch, seq):
    d = D_MODEL
    rot = lambda scale: functools.partial(_ep_rotary_split, head_dim=RET_DK, scale=scale)
    q = matmul(xn, w_in, (j,), col_offsets=(0,), n_cols=d, epilogue=rot(1.0),
               out_dtype=BF16, row_extras=(cos, sin), name="ret_q")
    k = matmul(xn, w_in, (j,), col_offsets=(d,), n_cols=d, epilogue=rot(RET_DK ** -0.5),
               out_dtype=BF16, row_extras=(cos, sin), name="ret_k")
    v = matmul(xn, w_in, (j,), col_offsets=(2 * d,), n_cols=d, epilogue=_ep_plain,
               out_dtype=BF16, name="ret_v")
    sg = matmul(xn, w_in, (j,), col_offsets=(3 * d,), n_cols=d, epilogue=_ep_silu,
                out_dtype=BF16, name="ret_g")
    return retention_core(q, k, v, sg, gn_g, j, batch, seq)


def _sgu_kernel(u_ref, v_ref, lng_ref, ws_ref, bs_ref, o_ref, wm_ref):
    c = SG_CHUNK
    row = lax.broadcasted_iota(jnp.int32, (c, c), 0)
    col = lax.broadcasted_iota(jnp.int32, (c, c), 1)
    causal = row >= col
    for g in range(SG_GROUPS):
        wm_ref[g] = jnp.where(causal, ws_ref[g], 0.0).astype(wm_ref.dtype)
    bs = bs_ref[...]
    lng = lng_ref[...]

    def chunk(n, carry):
        rows = pl.ds(pl.multiple_of(n * c, c), c)
        v = v_ref[rows, :].astype(F32)
        mu = jnp.mean(v, axis=-1, keepdims=True)
        var = jnp.mean(jnp.square(v - mu), axis=-1, keepdims=True)
        vn = ((v - mu) * lax.rsqrt(var + EPS) * lng).astype(BF16)
        for g in range(SG_GROUPS):
            cols = slice(g * SG_DG, (g + 1) * SG_DG)
            mixed = jnp.dot(wm_ref[g], vn[:, cols], preferred_element_type=F32) + bs[:, g:g + 1]
            o_ref[rows, cols] = (u_ref[rows, cols].astype(F32) * mixed).astype(o_ref.dtype)
        return carry

    lax.fori_loop(0, SG_ROWS // c, chunk, 0)


def sgu_core(uv, ln_g, w_s, b_s, j):
    m = uv.shape[0]
    d = D_MODEL
    c = SG_CHUNK
    assert m % SG_ROWS == 0 and SG_ROWS % c == 0
    return pl.pallas_call(
        _sgu_kernel,
        out_shape=jax.ShapeDtypeStruct((m, d), BF16),
        grid=(m // SG_ROWS,),
        in_specs=[pl.BlockSpec((SG_ROWS, d), lambda n: (n, 0)),
                  pl.BlockSpec((SG_ROWS, d), lambda n: (n, 1)),
                  pl.BlockSpec((None, 1, d), lambda n: (j, 0, 0)),
                  pl.BlockSpec((None, SG_GROUPS, c, c), lambda n: (j, 0, 0, 0)),
                  pl.BlockSpec((c, SG_GROUPS), lambda n: (0, 0))],
        out_specs=pl.BlockSpec((SG_ROWS, d), lambda n: (n, 0)),
        scratch_shapes=[pltpu.VMEM((SG_GROUPS, c, c), BF16)],
        compiler_params=_params(1),
        name="sgu_core",
    )(uv, uv, ln_g.reshape(ln_g.shape[0], 1, d), w_s, b_s[j].T)


def spatial_gating_mixer(xn, w_in, ln_g, w_s, b_s, j):
    d = D_MODEL
    uv = matmul(xn, w_in, (j,), col_offsets=(0,), n_cols=2 * d, epilogue=_ep_gelu,
                out_dtype=BF16, name="sg_in")
    return sgu_core(uv, ln_g, w_s, b_s, j)


MOBA_MASK = -3e38


def _moba_kernel(q_ref, k_ref, v_ref, o_ref, kaug_ref, vaug_ref, *, seq):
    blk = MOBA_BLOCK
    dh = MOBA_DH
    nblk = seq // blk
    n_sel = min(MOBA_TOPK, nblk - 1)
    pad_rows = 2 * SUBLANES

    vaug_ref[:, :dh] = v_ref[...]
    vaug_ref[:, dh:] = jnp.ones((seq, dh), vaug_ref.dtype)
    kaug_ref[:, :dh] = k_ref[...]
    key_blk = lax.broadcasted_iota(jnp.int32, (seq, dh), 0) // blk
    key_lane = lax.broadcasted_iota(jnp.int32, (seq, dh), 1)
    kaug_ref[:, dh:] = jnp.where(key_blk == key_lane, 1.0, 0.0).astype(kaug_ref.dtype)

    blk_row = lax.broadcasted_iota(jnp.int32, (pad_rows, seq), 0)
    blk_col = lax.broadcasted_iota(jnp.int32, (pad_rows, seq), 1) // blk
    member = jnp.where(blk_row == blk_col, 1.0, 0.0).astype(BF16)
    k_mean = jnp.dot(member, k_ref[...], preferred_element_type=F32) * (1.0 / blk)
    k_mean_hi = k_mean.astype(BF16)
    k_mean_lo = (k_mean - k_mean_hi.astype(F32)).astype(BF16)
    q_all = q_ref[...]
    gate_t = (lax.dot_general(k_mean_hi, q_all, _NT, preferred_element_type=F32)
              + lax.dot_general(k_mean_lo, q_all, _NT, preferred_element_type=F32))

    row = lax.broadcasted_iota(jnp.int32, (blk, LANE), 0)
    lane = lax.broadcasted_iota(jnp.int32, (blk, LANE), 1)
    blk_id = lax.broadcasted_iota(jnp.int32, (SUBLANES, blk), 0)

    def scores(i):
        q = q_ref[i * blk:(i + 1) * blk, :]
        nk = (i + 1) * blk
        if i > n_sel:
            g = gate_t[:SUBLANES, i * blk:(i + 1) * blk]
            bias = jnp.zeros((SUBLANES, blk), F32)
            for j in range(i):
                gj = g[j:j + 1, :]
                beats = ((g > gj) | ((g == gj) & (blk_id < j))) & (blk_id < i)
                rank = jnp.sum(jnp.where(beats, 1.0, 0.0), axis=0, keepdims=True)
                bias = jnp.where((blk_id == j) & (rank >= float(n_sel)), MOBA_MASK, bias)
            bias_t = jnp.concatenate([bias, jnp.zeros((LANE - SUBLANES, blk), F32)], axis=0)
            q_aug = jnp.concatenate([q, bias_t.T.astype(BF16)], axis=1)
            s = lax.dot_general(q_aug, kaug_ref[0:nk, :], _NT, preferred_element_type=F32)
        else:
            s = lax.dot_general(q, k_ref[0:nk, :], _NT, preferred_element_type=F32)
        chunks = [s[:, c * LANE:(c + 1) * LANE] for c in range(nk // LANE)]
        for t, c in enumerate(range(i * blk // LANE, nk // LANE)):
            chunks[c] = jnp.where(lane + t * LANE <= row, chunks[c], float("-inf"))
        return chunks

    def attend(i, chunks):
        nk = (i + 1) * blk
        m_el = chunks[0]
        for ch in chunks[1:]:
            m_el = jnp.maximum(m_el, ch)
        m = m_el.max(axis=-1, keepdims=True)
        p = jnp.concatenate([jnp.exp(ch - m).astype(BF16) for ch in chunks], axis=1)
        o_aug = jnp.dot(p, vaug_ref[0:nk, :], preferred_element_type=F32)
        o_ref[i * blk:(i + 1) * blk, :] = (o_aug[:, :dh] / o_aug[:, dh:]).astype(o_ref.dtype)

    pending = scores(0)
    for i in range(nblk):
        current = pending
        if i + 1 < nblk:
            pending = scores(i + 1)
        attend(i, current)


def moba_core(q, k, v, batch, seq):
    m, d = q.shape
    assert seq % MOBA_BLOCK == 0 and MOBA_DH == LANE and seq // MOBA_BLOCK <= SUBLANES
    blk = pl.BlockSpec((seq, MOBA_DH), lambda b, h: (b, h))
    return pl.pallas_call(
        functools.partial(_moba_kernel, seq=seq),
        out_shape=jax.ShapeDtypeStruct((m, d), BF16),
        grid=(batch, d // MOBA_DH),
        in_specs=[blk, blk, blk],
        out_specs=blk,
        scratch_shapes=[pltpu.VMEM((seq, 2 * MOBA_DH), BF16),
                        pltpu.VMEM((seq, 2 * MOBA_DH), BF16)],
        compiler_params=_params(2),
        name="moba_core",
    )(q, k, v)


def moba_mixer(xn, cosf, sins, w_in, j, batch, seq):
    d = D_MODEL
    rot = lambda scale: functools.partial(_ep_rotary_roll, head_dim=MOBA_DH, scale=scale)
    q = matmul(xn, w_in, (j,), col_offsets=(0,), n_cols=d, epilogue=rot(MOBA_DH ** -0.5),
               out_dtype=BF16, row_extras=(cosf, sins), name="moba_q")
    k = matmul(xn, w_in, (j,), col_offsets=(d,), n_cols=d, epilogue=rot(1.0),
               out_dtype=BF16, row_extras=(cosf, sins), name="moba_k")
    v = matmul(xn, w_in, (j,), col_offsets=(2 * d,), n_cols=d, epilogue=_ep_plain,
               out_dtype=BF16, name="moba_v")
    return moba_core(q, k, v, batch, seq)


def kernel(x, positions, norm_g, ffn_w_in, ffn_w_out, ret_w_in, ret_gn_g, ret_w_out,
           sg_w_in, sg_ln_g, sg_w_s, sg_b, sg_w_out, moba_w_in, moba_w_out):
    batch, seq, d = x.shape
    m = batch * seq
    pos = positions.reshape(m, 1).astype(F32)

    ret_inv = 1.0 / (10000.0 ** jnp.linspace(0.0, 1.0, RET_DK // 2, dtype=F32))
    ret_ang = pos * ret_inv
    ret_cos, ret_sin = jnp.cos(ret_ang), jnp.sin(ret_ang)
    moba_inv = 1.0 / (ROPE_THETA ** (jnp.arange(0, MOBA_DH, 2, dtype=F32) / MOBA_DH))
    moba_ang = pos * moba_inv
    mc, ms = jnp.cos(moba_ang), jnp.sin(moba_ang)
    moba_cosf = jnp.concatenate([mc, mc], axis=1)
    moba_sins = jnp.concatenate([-ms, ms], axis=1)

    n_gain = norm_g.shape[1]
    gains = norm_g.reshape(DEPTH * n_gain, 1, d)
    h = x.reshape(m, d)
    xn = pre_norm(h, gains, 0)
    for i in range(DEPTH):
        mix, j = i % N_MIXERS, i // N_MIXERS
        g0 = i * n_gain
        hid = ffn_hidden(xn, ffn_w_in, (i, 0))
        h, xn = out_proj_post_norm(hid, ffn_w_out, (i, 0), h, gains, g0 + 1, g0 + 2, 0.5, "ffn_out")
        if mix == 0:
            o = retention_mixer(xn, ret_cos, ret_sin, ret_w_in, ret_gn_g, j, batch, seq)
            w_out, name = ret_w_out, "ret_out"
        elif mix == 1:
            o = spatial_gating_mixer(xn, sg_w_in, sg_ln_g, sg_w_s, sg_b, j)
            w_out, name = sg_w_out, "sg_out"
        else:
            o = moba_mixer(xn, moba_cosf, moba_sins, moba_w_in, j, batch, seq)
            w_out, name = moba_w_out, "moba_out"
        h, xn = out_proj_post_norm(o, w_out, (j,), h, gains, g0 + 3, g0 + 4, 1.0, name)
        hid = ffn_hidden(xn, ffn_w_in, (i, 1))
        pre_next = g0 + n_gain if i + 1 < DEPTH else None
        h, xn = out_proj_post_norm(hid, ffn_w_out, (i, 1), h, gains, g0 + 5, pre_next, 0.5, "ffn_out")
    return h.reshape(batch, seq, d)
```

```python
import functools
import math

import jax
import jax.numpy as jnp
from jax import lax
from jax.experimental import pallas as pl
from jax.experimental.pallas import tpu as pltpu

F32 = jnp.float32
BF16 = jnp.bfloat16

D_MODEL = 4096
D_FF = 5632
DEPTH = 4
N_MIXERS = 3
RET_HEADS = 16
RET_DK = D_MODEL // RET_HEADS
RET_CHUNK = 128
RET_UNROLL = 16
SG_GROUPS = 16
SG_CHUNK = 128
SG_ROWS = 512
SG_DG = D_MODEL // SG_GROUPS
MOBA_HEADS = 32
MOBA_DH = D_MODEL // MOBA_HEADS
MOBA_BLOCK = 256
MOBA_TOPK = 3
ROPE_THETA = 10000.0
EPS = 1e-6

LANE = 128
SUBLANES = 8
VMEM_LIMIT = 56 * 1024 * 1024
ROW_TILE = 256
PN_GROUPS = 4
PN_PIECE = 16
PN_COLS = 128
MM_TM = 1024
MM_W_BLOCK_BYTES = 8 * 1024 * 1024


def _params(n_grid):
    return pltpu.CompilerParams(
        dimension_semantics=("arbitrary",) * n_grid, vmem_limit_bytes=VMEM_LIMIT)


def _rms(x, g):
    return x * lax.rsqrt(jnp.mean(x * x, axis=-1, keepdims=True) + EPS) * g


def _pre_norm_kernel(h_ref, g_ref, xn_ref):
    xn_ref[...] = _rms(h_ref[...], g_ref[...]).astype(BF16)


def _gain_spec(idx):
    return pl.BlockSpec((None, 1, D_MODEL), lambda *_: (idx, 0, 0))


def pre_norm(h, gains, g_idx):
    m, d = h.shape
    return pl.pallas_call(
        _pre_norm_kernel,
        out_shape=jax.ShapeDtypeStruct((m, d), BF16),
        grid=(m // ROW_TILE,),
        in_specs=[pl.BlockSpec((ROW_TILE, d), lambda i: (i, 0)), _gain_spec(g_idx)],
        out_specs=pl.BlockSpec((ROW_TILE, d), lambda i: (i, 0)),
        compiler_params=_params(1),
        name="pre_norm",
    )(h, gains)


def _post_norm_rows(y_ref, h_ref, gpost_ref, gpre_ref, hout_ref, xn_ref, coef):
    d = y_ref.shape[-1]
    col_chunks = [slice(c0, c0 + PN_COLS) for c0 in range(0, d, PN_COLS)]

    def mean_sq(ref):
        total = 0.0
        for cs in col_chunks:
            total = total + jnp.sum(jnp.square(ref[:, cs].astype(F32)), axis=-1, keepdims=True)
        return total * (1.0 / d)

    rs = lax.rsqrt(mean_sq(y_ref) + EPS)
    for cs in col_chunks:
        hout_ref[:, cs] = h_ref[:, cs] + coef * (y_ref[:, cs].astype(F32) * rs * gpost_ref[:, cs])
    if xn_ref is not None:
        rs = lax.rsqrt(mean_sq(hout_ref) + EPS)
        for cs in col_chunks:
            xn_ref[:, cs] = (hout_ref[:, cs] * rs * gpre_ref[:, cs]).astype(BF16)


def _split_norm_refs(refs, has_pre, n_carry):
    gpost_ref = refs[0]
    gpre_ref = refs[1] if has_pre else None
    return gpost_ref, gpre_ref, refs[1 + has_pre + n_carry:]


def _post_norm_kernel(y_ref, h_ref, *refs, coef, has_pre, n_carry):
    gpost_ref, gpre_ref, outs = _split_norm_refs(refs, has_pre, n_carry)
    _post_norm_rows(y_ref, h_ref, gpost_ref, gpre_ref, outs[0], outs[1] if has_pre else None, coef)


def _mm_kernel(*refs, n_parts, n_extra, epilogue):
    x_ref = refs[0]
    w_refs = refs[1:1 + n_parts]
    extra_refs = refs[1 + n_parts:1 + n_parts + n_extra]
    out_ref = refs[1 + n_parts + n_extra]
    x = x_ref[...]
    accs = [jnp.dot(x, w_ref[...].astype(BF16), preferred_element_type=F32) for w_ref in w_refs]
    out_ref[...] = epilogue(accs, extra_refs).astype(out_ref.dtype)


def _mm_col_tile(k, n_parts):
    tn = LANE
    while n_parts * k * (2 * tn) * 4 <= MM_W_BLOCK_BYTES:
        tn *= 2
    return tn


def matmul(x, w, lead, *, col_offsets, n_cols, epilogue, out_dtype, row_extras=(), name,
           row_tiles=None):
    m, k = x.shape
    tm = MM_TM
    tn = _mm_col_tile(k, len(col_offsets))
    assert m % tm == 0 and n_cols % tn == 0 and all(o % tn == 0 for o in col_offsets)
    assert w.shape[len(lead)] == k
    row0, n_row = (0, m // tm) if row_tiles is None else row_tiles
    in_specs = [pl.BlockSpec((tm, k), lambda i, j: (row0 + i, 0))]
    w_block = (None,) * len(lead) + (k, tn)
    for off in col_offsets:
        in_specs.append(pl.BlockSpec(w_block, lambda i, j, ob=off // tn: (*lead, 0, ob + j)))
    for e in row_extras:
        in_specs.append(pl.BlockSpec((tm, e.shape[1]), lambda i, j: (row0 + i, 0)))
    kern = functools.partial(_mm_kernel, n_parts=len(col_offsets), n_extra=len(row_extras),
                             epilogue=epilogue)
    return pl.pallas_call(
        kern,
        out_shape=jax.ShapeDtypeStruct((n_row * tm, n_cols), out_dtype),
        grid=(n_row, n_cols // tn),
        in_specs=in_specs,
        out_specs=pl.BlockSpec((tm, tn), lambda i, j: (i, j)),
        compiler_params=_params(2),
        name=name,
    )(x, *([w] * len(col_offsets)), *row_extras)


def _mm_post_kernel(x_ref, w_ref, yprev_ref, h_ref, *refs, coef, has_pre, n_carry):
    gpost_ref, gpre_ref, outs = _split_norm_refs(refs, has_pre, n_carry)
    for r in range(0, h_ref.shape[0], PN_PIECE):
        rows = slice(r, r + PN_PIECE)
        _post_norm_rows(yprev_ref.at[rows], h_ref.at[rows], gpost_ref, gpre_ref, outs[1].at[rows],
                        outs[2].at[rows] if has_pre else None, coef)
    y_ref = outs[0]
    y_ref[...] = jnp.dot(x_ref[...], w_ref[...].astype(BF16),
                         preferred_element_type=F32).astype(y_ref.dtype)


def out_proj_post_norm(x, w, lead, h, gains, post_idx, pre_idx, coef, name):
    m, k = x.shape
    d = D_MODEL
    tm = MM_TM
    tn = _mm_col_tile(k, 1)
    n_col = d // tn
    group_rows = m // PN_GROUPS
    tiles = group_rows // tm
    steps = tiles * n_col
    chunk = group_rows // steps
    assert group_rows % tm == 0 and group_rows % steps == 0 and chunk % (2 * SUBLANES) == 0
    assert group_rows % ROW_TILE == 0 and w.shape[len(lead)] == k
    has_pre = pre_idx is not None
    gain_specs = [_gain_spec(post_idx)] + ([_gain_spec(pre_idx)] if has_pre else [])
    gain_args = [gains] * len(gain_specs)
    carry_spec = pl.BlockSpec(memory_space=pl.ANY)
    norm_shapes = [jax.ShapeDtypeStruct((m, d), F32)] + (
        [jax.ShapeDtypeStruct((m, d), BF16)] if has_pre else [])
    w_block = (None,) * len(lead) + (k, tn)

    y_prev = matmul(x, w, lead, col_offsets=(0,), n_cols=d, epilogue=_ep_plain, out_dtype=BF16,
                    name=name, row_tiles=(0, tiles))
    carried = []
    for q in range(1, PN_GROUPS):
        chunk_map = lambda i, j, q=q: ((q - 1) * steps + i * n_col + j, 0)
        in_specs = [pl.BlockSpec((tm, k), lambda i, j, q=q: (q * tiles + i, 0)),
                    pl.BlockSpec(w_block, lambda i, j: (*lead, 0, j)),
                    pl.BlockSpec((chunk, d), lambda i, j: (i * n_col + j, 0)),
                    pl.BlockSpec((chunk, d), chunk_map)] + gain_specs
        outs = pl.pallas_call(
            functools.partial(_mm_post_kernel, coef=coef, has_pre=has_pre, n_carry=len(carried)),
            out_shape=[jax.ShapeDtypeStruct((group_rows, d), BF16)] + norm_shapes,
            grid=(tiles, n_col),
            in_specs=in_specs + [carry_spec] * len(carried),
            out_specs=[pl.BlockSpec((tm, tn), lambda i, j: (i, j))]
            + [pl.BlockSpec((chunk, d), chunk_map)] * len(norm_shapes),
            input_output_aliases={len(in_specs) + c: 1 + c for c in range(len(carried))},
            compiler_params=_params(2),
            name=name + "_post",
        )(x, w, y_prev, h, *gain_args, *carried)
        y_prev, carried = outs[0], list(outs[1:])

    first_row = (PN_GROUPS - 1) * (group_rows // ROW_TILE)
    row_map = lambda i: (first_row + i, 0)
    in_specs = [pl.BlockSpec((ROW_TILE, d), lambda i: (i, 0)),
                pl.BlockSpec((ROW_TILE, d), row_map)] + gain_specs
    outs = pl.pallas_call(
        functools.partial(_post_norm_kernel, coef=coef, has_pre=has_pre, n_carry=len(carried)),
        out_shape=norm_shapes,
        grid=(group_rows // ROW_TILE,),
        in_specs=in_specs + [carry_spec] * len(carried),
        out_specs=[pl.BlockSpec((ROW_TILE, d), row_map)] * len(norm_shapes),
        input_output_aliases={len(in_specs) + c: c for c in range(len(carried))},
        compiler_params=_params(1),
        name="post_norm_tail",
    )(y_prev, h, *gain_args, *carried)
    return outs[0], (outs[1] if has_pre else None)


def _ep_plain(accs, extras):
    return accs[0]


def _ep_swiglu(accs, extras):
    a, b = accs
    return a * jax.nn.sigmoid(a) * b


def _ep_silu(accs, extras):
    a = accs[0]
    return a * jax.nn.sigmoid(a)


def _ep_gelu(accs, extras):
    a = accs[0]
    return 0.5 * a * (1.0 + lax.erf(a * math.sqrt(0.5)))


def _ep_rotary_split(accs, extras, *, head_dim, scale):
    acc = accs[0]
    cos = extras[0][...]
    sin = extras[1][...]
    half = head_dim // 2
    outs = []
    for h0 in range(0, acc.shape[1], head_dim):
        t1 = acc[:, h0:h0 + half]
        t2 = acc[:, h0 + half:h0 + head_dim]
        outs.append(t1 * cos - t2 * sin)
        outs.append(t1 * sin + t2 * cos)
    out = jnp.concatenate(outs, axis=1)
    return out * scale if scale != 1.0 else out


def _ep_rotary_roll(accs, extras, *, head_dim, scale):
    acc = accs[0]
    cosf = extras[0][...]
    sins = extras[1][...]
    outs = []
    for h0 in range(0, acc.shape[1], head_dim):
        t = acc[:, h0:h0 + head_dim]
        outs.append(t * cosf + pltpu.roll(t, head_dim // 2, axis=1) * sins)
    out = jnp.concatenate(outs, axis=1)
    return out * scale if scale != 1.0 else out


def ffn_hidden(xn, w_in, lead):
    return matmul(xn, w_in, lead, col_offsets=(0, D_FF), n_cols=D_FF,
                  epilogue=_ep_swiglu, out_dtype=BF16, name="ffn_in")


_NT = (((1,), (1,)), ((), ()))
_TN = (((0,), (0,)), ((), ()))


def _retention_kernel(lg_ref, q_ref, k_ref, v_ref, sg_ref, gn_ref, o_ref, state_ref, *, seq):
    c = RET_CHUNK
    dk = RET_DK
    lg = lg_ref[0]
    row = lax.broadcasted_iota(jnp.int32, (c, c), 0)
    col = lax.broadcasted_iota(jnp.int32, (c, c), 1)
    dist = (row - col).astype(F32)
    intra_decay = jnp.where(dist >= 0, jnp.exp(lg[:, :c] * jnp.maximum(dist, 0.0)), 0.0)
    idx = lax.broadcasted_iota(jnp.int32, (c, dk), 0).astype(F32)
    q_decay = jnp.exp(lg * (idx + 1.0))
    k_decay = jnp.exp(lg * (c - 1.0 - idx))
    chunk_decay = jnp.exp(lg * float(c))
    gn = gn_ref[...]
    state_ref[...] = jnp.zeros_like(state_ref)

    def front(rows):
        qc = q_ref[rows, :]
        kc = k_ref[rows, :]
        vc = v_ref[rows, :]
        scores = lax.dot_general(qc, kc, _NT, preferred_element_type=F32) * intra_decay
        intra = jnp.dot(scores.astype(BF16), vc, preferred_element_type=F32)
        qd = (qc.astype(F32) * q_decay).astype(BF16)
        kd = (kc.astype(F32) * k_decay).astype(BF16)
        update = lax.dot_general(kd, vc, _TN, preferred_element_type=F32)
        return qd, intra, update

    def back(rows, qd, intra, update):
        state = state_ref[...]
        cross = jnp.dot(qd, state.astype(BF16), preferred_element_type=F32)
        state_ref[...] = state * chunk_decay + update
        o = intra + cross
        mu = jnp.mean(o, axis=-1, keepdims=True)
        var = jnp.mean(jnp.square(o - mu), axis=-1, keepdims=True)
        on = (o - mu) * lax.rsqrt(var + EPS) * gn
        o_ref[rows, :] = (sg_ref[rows, :].astype(F32) * on).astype(o_ref.dtype)

    def group(gi, carry):
        rows = lambda u: pl.ds(pl.multiple_of((gi * RET_UNROLL + u) * c, c), c)
        pending = front(rows(0))
        for u in range(RET_UNROLL):
            current = pending
            if u + 1 < RET_UNROLL:
                pending = front(rows(u + 1))
            back(rows(u), *current)
        return carry

    lax.fori_loop(0, seq // (c * RET_UNROLL), group, 0)


def retention_core(q, k, v, sg, gn_g, j, batch, seq):
    m, d = q.shape
    log_gamma = jnp.log1p(-jnp.exp2(-5.0 - jnp.arange(RET_HEADS, dtype=F32)))
    lg = jnp.broadcast_to(log_gamma[:, None, None], (RET_HEADS, 1, RET_DK))
    blk = pl.BlockSpec((seq, RET_DK), lambda b, h: (b, h))
    return pl.pallas_call(
        functools.partial(_retention_kernel, seq=seq),
        out_shape=jax.ShapeDtypeStruct((m, d), BF16),
        grid=(batch, RET_HEADS),
        in_specs=[pl.BlockSpec((1, 1, RET_DK), lambda b, h: (h, 0, 0)),
                  blk, blk, blk, blk,
                  pl.BlockSpec((None, 1, RET_DK), lambda b, h: (j, 0, h))],
        out_specs=blk,
        scratch_shapes=[pltpu.VMEM((RET_DK, RET_DK), F32)],
        compiler_params=_params(2),
        name="retention_core",
    )(lg, q, k, v, sg, gn_g.reshape(gn_g.shape[0], 1, d))


def retention_mixer(xn, cos, sin, w_in, gn_g, j, batch, seq):
    d = D_MODEL
    rot = lambda scale: functools.partial(_ep_rotary_split, head_dim=RET_DK, scale=scale)
    q = matmul(xn, w_in, (j,), col_offsets=(0,), n_cols=d, epilogue=rot(1.0),
               out_dtype=BF16, row_extras=(cos, sin), name="ret_q")
    k = matmul(xn, w_in, (j,), col_offsets=(d,), n_cols=d, epilogue=rot(RET_DK ** -0.5),
               out_dtype=BF16, row_extras=(cos, sin), name="ret_k")
    v = matmul(xn, w_in, (j,), col_offsets=(2 * d,), n_cols=d, epilogue=_ep_plain,
               out_dtype=BF16, name="ret_v")
    sg = matmul(xn, w_in, (j,), col_offsets=(3 * d,), n_cols=d, epilogue=_ep_silu,
                out_dtype=BF16, name="ret_g")
    return retention_core(q, k, v, sg, gn_g, j, batch, seq)


def _sgu_kernel(u_ref, v_ref, lng_ref, ws_ref, bs_ref, o_ref, wm_ref):
    c = SG_CHUNK
    row = lax.broadcasted_iota(jnp.int32, (c, c), 0)
    col = lax.broadcasted_iota(jnp.int32, (c, c), 1)
    causal = row >= col
    for g in range(SG_GROUPS):
        wm_ref[g] = jnp.where(causal, ws_ref[g], 0.0).astype(wm_ref.dtype)
    bs = bs_ref[...]
    lng = lng_ref[...]

    def chunk(n, carry):
        rows = pl.ds(pl.multiple_of(n * c, c), c)
        v = v_ref[rows, :].astype(F32)
        mu = jnp.mean(v, axis=-1, keepdims=True)
        var = jnp.mean(jnp.square(v - mu), axis=-1, keepdims=True)
        vn = ((v - mu) * lax.rsqrt(var + EPS) * lng).astype(BF16)
        for g in range(SG_GROUPS):
            cols = slice(g * SG_DG, (g + 1) * SG_DG)
            mixed = jnp.dot(wm_ref[g], vn[:, cols], preferred_element_type=F32) + bs[:, g:g + 1]
            o_ref[rows, cols] = (u_ref[rows, cols].astype(F32) * mixed).astype(o_ref.dtype)
        return carry

    lax.fori_loop(0, SG_ROWS // c, chunk, 0)


def sgu_core(uv, ln_g, w_s, b_s, j):
    m = uv.shape[0]
    d = D_MODEL
    c = SG_CHUNK
    assert m % SG_ROWS == 0 and SG_ROWS % c == 0
    return pl.pallas_call(
        _sgu_kernel,
        out_shape=jax.ShapeDtypeStruct((m, d), BF16),
        grid=(m // SG_ROWS,),
        in_specs=[pl.BlockSpec((SG_ROWS, d), lambda n: (n, 0)),
                  pl.BlockSpec((SG_ROWS, d), lambda n: (n, 1)),
                  pl.BlockSpec((None, 1, d), lambda n: (j, 0, 0)),
                  pl.BlockSpec((None, SG_GROUPS, c, c), lambda n: (j, 0, 0, 0)),
                  pl.BlockSpec((c, SG_GROUPS), lambda n: (0, 0))],
        out_specs=pl.BlockSpec((SG_ROWS, d), lambda n: (n, 0)),
        scratch_shapes=[pltpu.VMEM((SG_GROUPS, c, c), BF16)],
        compiler_params=_params(1),
        name="sgu_core",
    )(uv, uv, ln_g.reshape(ln_g.shape[0], 1, d), w_s, b_s[j].T)


def spatial_gating_mixer(xn, w_in, ln_g, w_s, b_s, j):
    d = D_MODEL
    uv = matmul(xn, w_in, (j,), col_offsets=(0,), n_cols=2 * d, epilogue=_ep_gelu,
                out_dtype=BF16, name="sg_in")
    return sgu_core(uv, ln_g, w_s, b_s, j)


MOBA_MASK = -3e38


def _moba_kernel(q_ref, k_ref, v_ref, o_ref, kaug_ref, vaug_ref, *, seq):
    blk = MOBA_BLOCK
    dh = MOBA_DH
    nblk = seq // blk
    n_sel = min(MOBA_TOPK, nblk - 1)
    pad_rows = 2 * SUBLANES

    vaug_ref[:, :dh] = v_ref[...]
    vaug_ref[:, dh:] = jnp.ones((seq, dh), vaug_ref.dtype)
    kaug_ref[:, :dh] = k_ref[...]
    key_blk = lax.broadcasted_iota(jnp.int32, (seq, dh), 0) // blk
    key_lane = lax.broadcasted_iota(jnp.int32, (seq, dh), 1)
    kaug_ref[:, dh:] = jnp.where(key_blk == key_lane, 1.0, 0.0).astype(kaug_ref.dtype)

    blk_row = lax.broadcasted_iota(jnp.int32, (pad_rows, seq), 0)
    blk_col = lax.broadcasted_iota(jnp.int32, (pad_rows, seq), 1) // blk
    member = jnp.where(blk_row == blk_col, 1.0, 0.0).astype(BF16)
    k_mean = jnp.dot(member, k_ref[...], preferred_element_type=F32) * (1.0 / blk)
    k_mean_hi = k_mean.astype(BF16)
    k_mean_lo = (k_mean - k_mean_hi.astype(F32)).astype(BF16)
    q_all = q_ref[...]
    gate_t = (lax.dot_general(k_mean_hi, q_all, _NT, preferred_element_type=F32)
              + lax.dot_general(k_mean_lo, q_all, _NT, preferred_element_type=F32))

    row = lax.broadcasted_iota(jnp.int32, (blk, LANE), 0)
    lane = lax.broadcasted_iota(jnp.int32, (blk, LANE), 1)
    blk_id = lax.broadcasted_iota(jnp.int32, (SUBLANES, blk), 0)

    def scores(i):
        q = q_ref[i * blk:(i + 1) * blk, :]
        nk = (i + 1) * blk
        if i > n_sel:
            g = gate_t[:SUBLANES, i * blk:(i + 1) * blk]
            bias = jnp.zeros((SUBLANES, blk), F32)
            for j in range(i):
                gj = g[j:j + 1, :]
                beats = ((g > gj) | ((g == gj) & (blk_id < j))) & (blk_id < i)
                rank = jnp.sum(jnp.where(beats, 1.0, 0.0), axis=0, keepdims=True)
                bias = jnp.where((blk_id == j) & (rank >= float(n_sel)), MOBA_MASK, bias)
            bias_t = jnp.concatenate([bias, jnp.zeros((LANE - SUBLANES, blk), F32)], axis=0)
            q_aug = jnp.concatenate([q, bias_t.T.astype(BF16)], axis=1)
            s = lax.dot_general(q_aug, kaug_ref[0:nk, :], _NT, preferred_element_type=F32)
        else:
            s = lax.dot_general(q, k_ref[0:nk, :], _NT, preferred_element_type=F32)
        chunks = [s[:, c * LANE:(c + 1) * LANE] for c in range(nk // LANE)]
        for t, c in enumerate(range(i * blk // LANE, nk // LANE)):
            chunks[c] = jnp.where(lane + t * LANE <= row, chunks[c], float("-inf"))
        return chunks

    def attend(i, chunks):
        nk = (i + 1) * blk
        m_el = chunks[0]
        for ch in chunks[1:]:
            m_el = jnp.maximum(m_el, ch)
        m = m_el.max(axis=-1, keepdims=True)
        p = jnp.concatenate([jnp.exp(ch - m).astype(BF16) for ch in chunks], axis=1)
        o_aug = jnp.dot(p, vaug_ref[0:nk, :], preferred_element_type=F32)
        o_ref[i * blk:(i + 1) * blk, :] = (o_aug[:, :dh] / o_aug[:, dh:]).astype(o_ref.dtype)

    pending = scores(0)
    for i in range(nblk):
        current = pending
        if i + 1 < nblk:
            pending = scores(i + 1)
        attend(i, current)


def moba_core(q, k, v, batch, seq):
    m, d = q.shape
    assert seq % MOBA_BLOCK == 0 and MOBA_DH == LANE and seq // MOBA_BLOCK <= SUBLANES
    blk = pl.BlockSpec((seq, MOBA_DH), lambda b, h: (b, h))
    return pl.pallas_call(
        functools.partial(_moba_kernel, seq=seq),
        out_shape=jax.ShapeDtypeStruct((m, d), BF16),
        grid=(batch, d // MOBA_DH),
        in_specs=[blk, blk, blk],
        out_specs=blk,
        scratch_shapes=[pltpu.VMEM((seq, 2 * MOBA_DH), BF16),
                        pltpu.VMEM((seq, 2 * MOBA_DH), BF16)],
        compiler_params=_params(2),
        name="moba_core",
    )(q, k, v)


def moba_mixer(xn, cosf, sins, w_in, j, batch, seq):
    d = D_MODEL
    rot = lambda scale: functools.partial(_ep_rotary_roll, head_dim=MOBA_DH, scale=scale)
    q = matmul(xn, w_in, (j,), col_offsets=(0,), n_cols=d, epilogue=rot(MOBA_DH ** -0.5),
               out_dtype=BF16, row_extras=(cosf, sins), name="moba_q")
    k = matmul(xn, w_in, (j,), col_offsets=(d,), n_cols=d, epilogue=rot(1.0),
               out_dtype=BF16, row_extras=(cosf, sins), name="moba_k")
    v = matmul(xn, w_in, (j,), col_offsets=(2 * d,), n_cols=d, epilogue=_ep_plain,
               out_dtype=BF16, name="moba_v")
    return moba_core(q, k, v, batch, seq)


def kernel(x, positions, norm_g, ffn_w_in, ffn_w_out, ret_w_in, ret_gn_g, ret_w_out,
           sg_w_in, sg_ln_g, sg_w_s, sg_b, sg_w_out, moba_w_in, moba_w_out):
    batch, seq, d = x.shape
    m = batch * seq
    pos = positions.reshape(m, 1).astype(F32)

    ret_inv = 1.0 / (10000.0 ** jnp.linspace(0.0, 1.0, RET_DK // 2, dtype=F32))
    ret_ang = pos * ret_inv
    ret_cos, ret_sin = jnp.cos(ret_ang), jnp.sin(ret_ang)
    moba_inv = 1.0 / (ROPE_THETA ** (jnp.arange(0, MOBA_DH, 2, dtype=F32) / MOBA_DH))
    moba_ang = pos * moba_inv
    mc, ms = jnp.cos(moba_ang), jnp.sin(moba_ang)
    moba_cosf = jnp.concatenate([mc, mc], axis=1)
    moba_sins = jnp.concatenate([-ms, ms], axis=1)

    n_gain = norm_g.shape[1]
    gains = norm_g.reshape(DEPTH * n_gain, 1, d)
    h = x.reshape(m, d)
    xn = pre_norm(h, gains, 0)
    for i in range(DEPTH):
        mix, j = i % N_MIXERS, i // N_MIXERS
        g0 = i * n_gain
        hid = ffn_hidden(xn, ffn_w_in, (i, 0))
        h, xn = out_proj_post_norm(hid, ffn_w_out, (i, 0), h, gains, g0 + 1, g0 + 2, 0.5, "ffn_out")
        if mix == 0:
            o = retention_mixer(xn, ret_cos, ret_sin, ret_w_in, ret_gn_g, j, batch, seq)
            w_out, name = ret_w_out, "ret_out"
        elif mix == 1:
            o = spatial_gating_mixer(xn, sg_w_in, sg_ln_g, sg_w_s, sg_b, j)
            w_out, name = sg_w_out, "sg_out"
        else:
            o = moba_mixer(xn, moba_cosf, moba_sins, moba_w_in, j, batch, seq)
            w_out, name = moba_w_out, "moba_out"
        h, xn = out_proj_post_norm(o, w_out, (j,), h, gains, g0 + 3, g0 + 4, 1.0, name)
        hid = ffn_hidden(xn, ffn_w_in, (i, 1))
        pre_next = g0 + n_gain if i + 1 < DEPTH else None
        h, xn = out_proj_post_norm(hid, ffn_w_out, (i, 1), h, gains, g0 + 5, pre_next, 0.5, "ffn_out")
    return h.reshape(batch, seq, d)
```

```python
import functools
import math

import jax
import jax.numpy as jnp
from jax import lax
from jax.experimental import pallas as pl
from jax.experimental.pallas import tpu as pltpu

F32 = jnp.float32
BF16 = jnp.bfloat16

D_MODEL = 4096
D_FF = 5632
DEPTH = 4
N_MIXERS = 3
RET_HEADS = 16
RET_DK = D_MODEL // RET_HEADS
RET_CHUNK = 128
RET_UNROLL = 16
SG_GROUPS = 16
SG_CHUNK = 128
SG_ROWS = 512
SG_DG = D_MODEL // SG_GROUPS
MOBA_HEADS = 32
MOBA_DH = D_MODEL // MOBA_HEADS
MOBA_BLOCK = 256
MOBA_TOPK = 3
ROPE_THETA = 10000.0
EPS = 1e-6

LANE = 128
SUBLANES = 8
VMEM_LIMIT = 56 * 1024 * 1024
ROW_TILE = 256
PN_GROUPS = 4
PN_PIECE = 16
PN_COLS = 128
MM_TM = 1024
MM_W_BLOCK_BYTES = 8 * 1024 * 1024


def _params(n_grid):
    return pltpu.CompilerParams(
        dimension_semantics=("arbitrary",) * n_grid, vmem_limit_bytes=VMEM_LIMIT)


def _rms(x, g):
    return x * lax.rsqrt(jnp.mean(x * x, axis=-1, keepdims=True) + EPS) * g


def _pre_norm_kernel(h_ref, g_ref, xn_ref, hcopy_ref):
    h = h_ref[...]
    xn_ref[...] = _rms(h, g_ref[...]).astype(BF16)
    hcopy_ref[...] = h


def _gain_spec(idx):
    return pl.BlockSpec((None, 1, D_MODEL), lambda *_: (idx, 0, 0))


def pre_norm(h, gains, g_idx):
    m, d = h.shape
    row = pl.BlockSpec((ROW_TILE, d), lambda i: (i, 0))
    return pl.pallas_call(
        _pre_norm_kernel,
        out_shape=(jax.ShapeDtypeStruct((m, d), BF16), jax.ShapeDtypeStruct((m, d), F32)),
        grid=(m // ROW_TILE,),
        in_specs=[row, _gain_spec(g_idx)],
        out_specs=(row, row),
        compiler_params=_params(1),
        name="pre_norm",
    )(h, gains)


def _post_norm_rows(y_ref, h_ref, gpost_ref, gpre_ref, hout_ref, xn_ref, coef):
    d = y_ref.shape[-1]
    col_chunks = [slice(c0, c0 + PN_COLS) for c0 in range(0, d, PN_COLS)]

    def mean_sq(ref):
        total = 0.0
        for cs in col_chunks:
            total = total + jnp.sum(jnp.square(ref[:, cs].astype(F32)), axis=-1, keepdims=True)
        return total * (1.0 / d)

    rs = lax.rsqrt(mean_sq(y_ref) + EPS)
    for cs in col_chunks:
        hout_ref[:, cs] = h_ref[:, cs] + coef * (y_ref[:, cs].astype(F32) * rs * gpost_ref[:, cs])
    if xn_ref is not None:
        rs = lax.rsqrt(mean_sq(hout_ref) + EPS)
        for cs in col_chunks:
            xn_ref[:, cs] = (hout_ref[:, cs] * rs * gpre_ref[:, cs]).astype(BF16)


def _split_norm_refs(refs, has_pre):
    gpost_ref = refs[0]
    gpre_ref = refs[1] if has_pre else None
    return gpost_ref, gpre_ref, refs[1 + 2 * has_pre:]


def _post_norm_kernel(y_ref, h_ref, *refs, coef, has_pre):
    gpost_ref, gpre_ref, outs = _split_norm_refs(refs, has_pre)
    _post_norm_rows(y_ref, h_ref, gpost_ref, gpre_ref, outs[0], outs[1] if has_pre else None, coef)


def _mm_kernel(*refs, n_parts, n_extra, epilogue):
    x_ref = refs[0]
    w_refs = refs[1:1 + n_parts]
    extra_refs = refs[1 + n_parts:1 + n_parts + n_extra]
    out_ref = refs[1 + n_parts + n_extra]
    x = x_ref[...]
    accs = [jnp.dot(x, w_ref[...].astype(BF16), preferred_element_type=F32) for w_ref in w_refs]
    out_ref[...] = epilogue(accs, extra_refs).astype(out_ref.dtype)


def _mm_col_tile(k, n_parts):
    tn = LANE
    while n_parts * k * (2 * tn) * 4 <= MM_W_BLOCK_BYTES:
        tn *= 2
    return tn


def matmul(x, w, lead, *, col_offsets, n_cols, epilogue, out_dtype, row_extras=(), name,
           row_tiles=None):
    m, k = x.shape
    tm = MM_TM
    tn = _mm_col_tile(k, len(col_offsets))
    assert m % tm == 0 and n_cols % tn == 0 and all(o % tn == 0 for o in col_offsets)
    assert w.shape[len(lead)] == k
    row0, n_row = (0, m // tm) if row_tiles is None else row_tiles
    in_specs = [pl.BlockSpec((tm, k), lambda i, j: (row0 + i, 0))]
    w_block = (None,) * len(lead) + (k, tn)
    for off in col_offsets:
        in_specs.append(pl.BlockSpec(w_block, lambda i, j, ob=off // tn: (*lead, 0, ob + j)))
    for e in row_extras:
        in_specs.append(pl.BlockSpec((tm, e.shape[1]), lambda i, j: (row0 + i, 0)))
    kern = functools.partial(_mm_kernel, n_parts=len(col_offsets), n_extra=len(row_extras),
                             epilogue=epilogue)
    return pl.pallas_call(
        kern,
        out_shape=jax.ShapeDtypeStruct((n_row * tm, n_cols), out_dtype),
        grid=(n_row, n_cols // tn),
        in_specs=in_specs,
        out_specs=pl.BlockSpec((tm, tn), lambda i, j: (i, j)),
        compiler_params=_params(2),
        name=name,
    )(x, *([w] * len(col_offsets)), *row_extras)


def _mm_post_kernel(x_ref, w_ref, yprev_ref, h_ref, *refs, coef, has_pre):
    gpost_ref, gpre_ref, outs = _split_norm_refs(refs, has_pre)
    for r in range(0, h_ref.shape[0], PN_PIECE):
        rows = slice(r, r + PN_PIECE)
        _post_norm_rows(yprev_ref.at[rows], h_ref.at[rows], gpost_ref, gpre_ref, outs[1].at[rows],
                        outs[2].at[rows] if has_pre else None, coef)
    y_ref = outs[0]
    y_ref[...] = jnp.dot(x_ref[...], w_ref[...].astype(BF16),
                         preferred_element_type=F32).astype(y_ref.dtype)


def out_proj_post_norm(x, w, lead, h, xn, gains, post_idx, pre_idx, coef, name):
    m, k = x.shape
    d = D_MODEL
    tm = MM_TM
    tn = _mm_col_tile(k, 1)
    n_col = d // tn
    group_rows = m // PN_GROUPS
    tiles = group_rows // tm
    steps = tiles * n_col
    chunk = group_rows // steps
    assert group_rows % tm == 0 and group_rows % steps == 0 and chunk % PN_PIECE == 0
    assert group_rows % ROW_TILE == 0 and w.shape[len(lead)] == k and d % PN_COLS == 0
    has_pre = pre_idx is not None
    gain_specs = [_gain_spec(post_idx)] + ([_gain_spec(pre_idx)] if has_pre else [])
    gain_args = [gains] * len(gain_specs)
    xn_spec = [pl.BlockSpec(memory_space=pl.ANY)] * has_pre
    xn_arg = [xn] * has_pre
    w_block = (None,) * len(lead) + (k, tn)

    def norm_outputs(block, index_map):
        shapes = [jax.ShapeDtypeStruct((m, d), F32)] + [jax.ShapeDtypeStruct((m, d), BF16)] * has_pre
        return shapes, [pl.BlockSpec(block, index_map)] * len(shapes)

    y_prev = matmul(x, w, lead, col_offsets=(0,), n_cols=d, epilogue=_ep_plain, out_dtype=BF16,
                    name=name, row_tiles=(0, tiles))
    for q in range(1, PN_GROUPS):
        chunk_map = lambda i, j, q=q: ((q - 1) * steps + i * n_col + j, 0)
        in_specs = [pl.BlockSpec((tm, k), lambda i, j, q=q: (q * tiles + i, 0)),
                    pl.BlockSpec(w_block, lambda i, j: (*lead, 0, j)),
                    pl.BlockSpec((chunk, d), lambda i, j: (i * n_col + j, 0)),
                    pl.BlockSpec((chunk, d), chunk_map)] + gain_specs + xn_spec
        norm_shapes, norm_specs = norm_outputs((chunk, d), chunk_map)
        aliases = {3: 1, **({len(in_specs) - 1: 2} if has_pre else {})}
        outs = pl.pallas_call(
            functools.partial(_mm_post_kernel, coef=coef, has_pre=has_pre),
            out_shape=[jax.ShapeDtypeStruct((group_rows, d), BF16)] + norm_shapes,
            grid=(tiles, n_col),
            in_specs=in_specs,
            out_specs=[pl.BlockSpec((tm, tn), lambda i, j: (i, j))] + norm_specs,
            input_output_aliases=aliases,
            compiler_params=_params(2),
            name=name + "_post",
        )(x, w, y_prev, h, *gain_args, *xn_arg)
        y_prev, h = outs[0], outs[1]
        xn_arg = [outs[2]] if has_pre else []

    first_row = (PN_GROUPS - 1) * (group_rows // ROW_TILE)
    row_map = lambda i: (first_row + i, 0)
    in_specs = [pl.BlockSpec((ROW_TILE, d), lambda i: (i, 0)),
                pl.BlockSpec((ROW_TILE, d), row_map)] + gain_specs + xn_spec
    norm_shapes, norm_specs = norm_outputs((ROW_TILE, d), row_map)
    aliases = {1: 0, **({len(in_specs) - 1: 1} if has_pre else {})}
    outs = pl.pallas_call(
        functools.partial(_post_norm_kernel, coef=coef, has_pre=has_pre),
        out_shape=norm_shapes,
        grid=(group_rows // ROW_TILE,),
        in_specs=in_specs,
        out_specs=norm_specs,
        input_output_aliases=aliases,
        compiler_params=_params(1),
        name="post_norm_tail",
    )(y_prev, h, *gain_args, *xn_arg)
    return outs[0], (outs[1] if has_pre else None)


def _ep_plain(accs, extras):
    return accs[0]


def _ep_swiglu(accs, extras):
    a, b = accs
    return a * jax.nn.sigmoid(a) * b


def _ep_silu(accs, extras):
    a = accs[0]
    return a * jax.nn.sigmoid(a)


def _ep_gelu(accs, extras):
    a = accs[0]
    return 0.5 * a * (1.0 + lax.erf(a * math.sqrt(0.5)))


def _ep_rotary_split(accs, extras, *, head_dim, scale):
    acc = accs[0]
    cos = extras[0][...]
    sin = extras[1][...]
    half = head_dim // 2
    outs = []
    for h0 in range(0, acc.shape[1], head_dim):
        t1 = acc[:, h0:h0 + half]
        t2 = acc[:, h0 + half:h0 + head_dim]
        outs.append(t1 * cos - t2 * sin)
        outs.append(t1 * sin + t2 * cos)
    out = jnp.concatenate(outs, axis=1)
    return out * scale if scale != 1.0 else out


def _ep_rotary_roll(accs, extras, *, head_dim, scale):
    acc = accs[0]
    cosf = extras[0][...]
    sins = extras[1][...]
    outs = []
    for h0 in range(0, acc.shape[1], head_dim):
        t = acc[:, h0:h0 + head_dim]
        outs.append(t * cosf + pltpu.roll(t, head_dim // 2, axis=1) * sins)
    out = jnp.concatenate(outs, axis=1)
    return out * scale if scale != 1.0 else out


def ffn_hidden(xn, w_in, lead):
    return matmul(xn, w_in, lead, col_offsets=(0, D_FF), n_cols=D_FF,
                  epilogue=_ep_swiglu, out_dtype=BF16, name="ffn_in")


_NT = (((1,), (1,)), ((), ()))
_TN = (((0,), (0,)), ((), ()))


def _retention_kernel(lg_ref, q_ref, k_ref, v_ref, sg_ref, gn_ref, o_ref, state_ref, *, seq):
    c = RET_CHUNK
    dk = RET_DK
    lg = lg_ref[0]
    row = lax.broadcasted_iota(jnp.int32, (c, c), 0)
    col = lax.broadcasted_iota(jnp.int32, (c, c), 1)
    dist = (row - col).astype(F32)
    intra_decay = jnp.where(dist >= 0, jnp.exp(lg[:, :c] * jnp.maximum(dist, 0.0)), 0.0)
    idx = lax.broadcasted_iota(jnp.int32, (c, dk), 0).astype(F32)
    q_decay = jnp.exp(lg * (idx + 1.0))
    k_decay = jnp.exp(lg * (c - 1.0 - idx))
    chunk_decay = jnp.exp(lg * float(c))
    gn = gn_ref[...]
    state_ref[...] = jnp.zeros_like(state_ref)

    def front(rows):
        qc = q_ref[rows, :]
        kc = k_ref[rows, :]
        vc = v_ref[rows, :]
        scores = lax.dot_general(qc, kc, _NT, preferred_element_type=F32) * intra_decay
        intra = jnp.dot(scores.astype(BF16), vc, preferred_element_type=F32)
        qd = (qc.astype(F32) * q_decay).astype(BF16)
        kd = (kc.astype(F32) * k_decay).astype(BF16)
        update = lax.dot_general(kd, vc, _TN, preferred_element_type=F32)
        return qd, intra, update

    def back(rows, qd, intra, update):
        state = state_ref[...]
        cross = jnp.dot(qd, state.astype(BF16), preferred_element_type=F32)
        state_ref[...] = state * chunk_decay + update
        o = intra + cross
        mu = jnp.mean(o, axis=-1, keepdims=True)
        var = jnp.mean(jnp.square(o - mu), axis=-1, keepdims=True)
        on = (o - mu) * lax.rsqrt(var + EPS) * gn
        o_ref[rows, :] = (sg_ref[rows, :].astype(F32) * on).astype(o_ref.dtype)

    def group(gi, carry):
        rows = lambda u: pl.ds(pl.multiple_of((gi * RET_UNROLL + u) * c, c), c)
        pending = front(rows(0))
        for u in range(RET_UNROLL):
            current = pending
            if u + 1 < RET_UNROLL:
                pending = front(rows(u + 1))
            back(rows(u), *current)
        return carry

    lax.fori_loop(0, seq // (c * RET_UNROLL), group, 0)


def retention_core(q, k, v, sg, gn_g, j, batch, seq):
    m, d = q.shape
    log_gamma = jnp.log1p(-jnp.exp2(-5.0 - jnp.arange(RET_HEADS, dtype=F32)))
    lg = jnp.broadcast_to(log_gamma[:, None, None], (RET_HEADS, 1, RET_DK))
    blk = pl.BlockSpec((seq, RET_DK), lambda b, h: (b, h))
    return pl.pallas_call(
        functools.partial(_retention_kernel, seq=seq),
        out_shape=jax.ShapeDtypeStruct((m, d), BF16),
        grid=(batch, RET_HEADS),
        in_specs=[pl.BlockSpec((1, 1, RET_DK), lambda b, h: (h, 0, 0)),
                  blk, blk, blk, blk,
                  pl.BlockSpec((None, 1, RET_DK), lambda b, h: (j, 0, h))],
        out_specs=blk,
        scratch_shapes=[pltpu.VMEM((RET_DK, RET_DK), F32)],
        compiler_params=_params(2),
        name="retention_core",
    )(lg, q, k, v, sg, gn_g.reshape(gn_g.shape[0], 1, d))


def retention_mixer(xn, cos, sin, w_in, gn_g, j, batch, seq):
    d = D_MODEL
    rot = lambda scale: functools.partial(_ep_rotary_split, head_dim=RET_DK, scale=scale)
    q = matmul(xn, w_in, (j,), col_offsets=(0,), n_cols=d, epilogue=rot(1.0),
               out_dtype=BF16, row_extras=(cos, sin), name="ret_q")
    k = matmul(xn, w_in, (j,), col_offsets=(d,), n_cols=d, epilogue=rot(RET_DK ** -0.5),
               out_dtype=BF16, row_extras=(cos, sin), name="ret_k")
    v = matmul(xn, w_in, (j,), col_offsets=(2 * d,), n_cols=d, epilogue=_ep_plain,
               out_dtype=BF16, name="ret_v")
    sg = matmul(xn, w_in, (j,), col_offsets=(3 * d,), n_cols=d, epilogue=_ep_silu,
                out_dtype=BF16, name="ret_g")
    return retention_core(q, k, v, sg, gn_g, j, batch, seq)


def _sgu_kernel(u_ref, v_ref, lng_ref, ws_ref, bs_ref, o_ref, wm_ref):
    c = SG_CHUNK
    row = lax.broadcasted_iota(jnp.int32, (c, c), 0)
    col = lax.broadcasted_iota(jnp.int32, (c, c), 1)
    causal = row >= col
    for g in range(SG_GROUPS):
        wm_ref[g] = jnp.where(causal, ws_ref[g], 0.0).astype(wm_ref.dtype)
    bs = bs_ref[...]
    lng = lng_ref[...]

    def chunk(n, carry):
        rows = pl.ds(pl.multiple_of(n * c, c), c)
        v = v_ref[rows, :].astype(F32)
        mu = jnp.mean(v, axis=-1, keepdims=True)
        var = jnp.mean(jnp.square(v - mu), axis=-1, keepdims=True)
        vn = ((v - mu) * lax.rsqrt(var + EPS) * lng).astype(BF16)
        for g in range(SG_GROUPS):
            cols = slice(g * SG_DG, (g + 1) * SG_DG)
            mixed = jnp.dot(wm_ref[g], vn[:, cols], preferred_element_type=F32) + bs[:, g:g + 1]
            o_ref[rows, cols] = (u_ref[rows, cols].astype(F32) * mixed).astype(o_ref.dtype)
        return carry

    lax.fori_loop(0, SG_ROWS // c, chunk, 0)


def sgu_core(uv, ln_g, w_s, b_s, j):
    m = uv.shape[0]
    d = D_MODEL
    c = SG_CHUNK
    assert m % SG_ROWS == 0 and SG_ROWS % c == 0
    return pl.pallas_call(
        _sgu_kernel,
        out_shape=jax.ShapeDtypeStruct((m, d), BF16),
        grid=(m // SG_ROWS,),
        in_specs=[pl.BlockSpec((SG_ROWS, d), lambda n: (n, 0)),
                  pl.BlockSpec((SG_ROWS, d), lambda n: (n, 1)),
                  pl.BlockSpec((None, 1, d), lambda n: (j, 0, 0)),
                  pl.BlockSpec((None, SG_GROUPS, c, c), lambda n: (j, 0, 0, 0)),
                  pl.BlockSpec((c, SG_GROUPS), lambda n: (0, 0))],
        out_specs=pl.BlockSpec((SG_ROWS, d), lambda n: (n, 0)),
        scratch_shapes=[pltpu.VMEM((SG_GROUPS, c, c), BF16)],
        compiler_params=_params(1),
        name="sgu_core",
    )(uv, uv, ln_g.reshape(ln_g.shape[0], 1, d), w_s, b_s[j].T)


def spatial_gating_mixer(xn, w_in, ln_g, w_s, b_s, j):
    d = D_MODEL
    uv = matmul(xn, w_in, (j,), col_offsets=(0,), n_cols=2 * d, epilogue=_ep_gelu,
                out_dtype=BF16, name="sg_in")
    return sgu_core(uv, ln_g, w_s, b_s, j)


MOBA_MASK = -3e38


def _moba_kernel(q_ref, k_ref, v_ref, o_ref, kaug_ref, vaug_ref, *, seq):
    blk = MOBA_BLOCK
    dh = MOBA_DH
    nblk = seq // blk
    n_sel = min(MOBA_TOPK, nblk - 1)
    pad_rows = 2 * SUBLANES

    vaug_ref[:, :dh] = v_ref[...]
    vaug_ref[:, dh:] = jnp.ones((seq, dh), vaug_ref.dtype)
    kaug_ref[:, :dh] = k_ref[...]
    key_blk = lax.broadcasted_iota(jnp.int32, (seq, dh), 0) // blk
    key_lane = lax.broadcasted_iota(jnp.int32, (seq, dh), 1)
    kaug_ref[:, dh:] = jnp.where(key_blk == key_lane, 1.0, 0.0).astype(kaug_ref.dtype)

    blk_row = lax.broadcasted_iota(jnp.int32, (pad_rows, seq), 0)
    blk_col = lax.broadcasted_iota(jnp.int32, (pad_rows, seq), 1) // blk
    member = jnp.where(blk_row == blk_col, 1.0, 0.0).astype(BF16)
    k_mean = jnp.dot(member, k_ref[...], preferred_element_type=F32) * (1.0 / blk)
    k_mean_hi = k_mean.astype(BF16)
    k_mean_lo = (k_mean - k_mean_hi.astype(F32)).astype(BF16)
    q_all = q_ref[...]
    gate_t = (lax.dot_general(k_mean_hi, q_all, _NT, preferred_element_type=F32)
              + lax.dot_general(k_mean_lo, q_all, _NT, preferred_element_type=F32))

    row = lax.broadcasted_iota(jnp.int32, (blk, LANE), 0)
    lane = lax.broadcasted_iota(jnp.int32, (blk, LANE), 1)
    blk_id = lax.broadcasted_iota(jnp.int32, (SUBLANES, blk), 0)

    def scores(i):
        q = q_ref[i * blk:(i + 1) * blk, :]
        nk = (i + 1) * blk
        if i > n_sel:
            g = gate_t[:SUBLANES, i * blk:(i + 1) * blk]
            bias = jnp.zeros((SUBLANES, blk), F32)
            for j in range(i):
                gj = g[j:j + 1, :]
                beats = ((g > gj) | ((g == gj) & (blk_id < j))) & (blk_id < i)
                rank = jnp.sum(jnp.where(beats, 1.0, 0.0), axis=0, keepdims=True)
                bias = jnp.where((blk_id == j) & (rank >= float(n_sel)), MOBA_MASK, bias)
            bias_t = jnp.concatenate([bias, jnp.zeros((LANE - SUBLANES, blk), F32)], axis=0)
            q_aug = jnp.concatenate([q, bias_t.T.astype(BF16)], axis=1)
            s = lax.dot_general(q_aug, kaug_ref[0:nk, :], _NT, preferred_element_type=F32)
        else:
            s = lax.dot_general(q, k_ref[0:nk, :], _NT, preferred_element_type=F32)
        chunks = [s[:, c * LANE:(c + 1) * LANE] for c in range(nk // LANE)]
        for t, c in enumerate(range(i * blk // LANE, nk // LANE)):
            chunks[c] = jnp.where(lane + t * LANE <= row, chunks[c], float("-inf"))
        return chunks

    def attend(i, chunks):
        nk = (i + 1) * blk
        m_el = chunks[0]
        for ch in chunks[1:]:
            m_el = jnp.maximum(m_el, ch)
        m = m_el.max(axis=-1, keepdims=True)
        p = jnp.concatenate([jnp.exp(ch - m).astype(BF16) for ch in chunks], axis=1)
        o_aug = jnp.dot(p, vaug_ref[0:nk, :], preferred_element_type=F32)
        o_ref[i * blk:(i + 1) * blk, :] = (o_aug[:, :dh] / o_aug[:, dh:]).astype(o_ref.dtype)

    pending = scores(0)
    for i in range(nblk):
        current = pending
        if i + 1 < nblk:
            pending = scores(i + 1)
        attend(i, current)


def moba_core(q, k, v, batch, seq):
    m, d = q.shape
    assert seq % MOBA_BLOCK == 0 and MOBA_DH == LANE and seq // MOBA_BLOCK <= SUBLANES
    blk = pl.BlockSpec((seq, MOBA_DH), lambda b, h: (b, h))
    return pl.pallas_call(
        functools.partial(_moba_kernel, seq=seq),
        out_shape=jax.ShapeDtypeStruct((m, d), BF16),
        grid=(batch, d // MOBA_DH),
        in_specs=[blk, blk, blk],
        out_specs=blk,
        scratch_shapes=[pltpu.VMEM((seq, 2 * MOBA_DH), BF16),
                        pltpu.VMEM((seq, 2 * MOBA_DH), BF16)],
        compiler_params=_params(2),
        name="moba_core",
    )(q, k, v)


def moba_mixer(xn, cosf, sins, w_in, j, batch, seq):
    d = D_MODEL
    rot = lambda scale: functools.partial(_ep_rotary_roll, head_dim=MOBA_DH, scale=scale)
    q = matmul(xn, w_in, (j,), col_offsets=(0,), n_cols=d, epilogue=rot(MOBA_DH ** -0.5),
               out_dtype=BF16, row_extras=(cosf, sins), name="moba_q")
    k = matmul(xn, w_in, (j,), col_offsets=(d,), n_cols=d, epilogue=rot(1.0),
               out_dtype=BF16, row_extras=(cosf, sins), name="moba_k")
    v = matmul(xn, w_in, (j,), col_offsets=(2 * d,), n_cols=d, epilogue=_ep_plain,
               out_dtype=BF16, name="moba_v")
    return moba_core(q, k, v, batch, seq)


def kernel(x, positions, norm_g, ffn_w_in, ffn_w_out, ret_w_in, ret_gn_g, ret_w_out,
           sg_w_in, sg_ln_g, sg_w_s, sg_b, sg_w_out, moba_w_in, moba_w_out):
    batch, seq, d = x.shape
    m = batch * seq
    pos = positions.reshape(m, 1).astype(F32)

    ret_inv = 1.0 / (10000.0 ** jnp.linspace(0.0, 1.0, RET_DK // 2, dtype=F32))
    ret_ang = pos * ret_inv
    ret_cos, ret_sin = jnp.cos(ret_ang), jnp.sin(ret_ang)
    moba_inv = 1.0 / (ROPE_THETA ** (jnp.arange(0, MOBA_DH, 2, dtype=F32) / MOBA_DH))
    moba_ang = pos * moba_inv
    mc, ms = jnp.cos(moba_ang), jnp.sin(moba_ang)
    moba_cosf = jnp.concatenate([mc, mc], axis=1)
    moba_sins = jnp.concatenate([-ms, ms], axis=1)

    n_gain = norm_g.shape[1]
    gains = norm_g.reshape(DEPTH * n_gain, 1, d)
    h = x.reshape(m, d)
    xn, h = pre_norm(h, gains, 0)
    for i in range(DEPTH):
        mix, j = i % N_MIXERS, i // N_MIXERS
        g0 = i * n_gain
        hid = ffn_hidden(xn, ffn_w_in, (i, 0))
        h, xn = out_proj_post_norm(hid, ffn_w_out, (i, 0), h, xn, gains, g0 + 1, g0 + 2, 0.5, "ffn_out")
        if mix == 0:
            o = retention_mixer(xn, ret_cos, ret_sin, ret_w_in, ret_gn_g, j, batch, seq)
            w_out, name = ret_w_out, "ret_out"
        elif mix == 1:
            o = spatial_gating_mixer(xn, sg_w_in, sg_ln_g, sg_w_s, sg_b, j)
            w_out, name = sg_w_out, "sg_out"
        else:
            o = moba_mixer(xn, moba_cosf, moba_sins, moba_w_in, j, batch, seq)
            w_out, name = moba_w_out, "moba_out"
        h, xn = out_proj_post_norm(o, w_out, (j,), h, xn, gains, g0 + 3, g0 + 4, 1.0, name)
        hid = ffn_hidden(xn, ffn_w_in, (i, 1))
        pre_next = g0 + n_gain if i + 1 < DEPTH else None
        h, xn = out_proj_post_norm(hid, ffn_w_out, (i, 1), h, xn, gains, g0 + 5, pre_next, 0.5, "ffn_out")
    return h.reshape(batch, seq, d)
```

```python
import functools
import math

import jax
import jax.numpy as jnp
from jax import lax
from jax.experimental import pallas as pl
from jax.experimental.pallas import tpu as pltpu

F32 = jnp.float32
BF16 = jnp.bfloat16

D_MODEL = 4096
D_FF = 5632
DEPTH = 4
N_MIXERS = 3
RET_HEADS = 16
RET_DK = D_MODEL // RET_HEADS
RET_CHUNK = 128
RET_UNROLL = 16
SG_GROUPS = 16
SG_CHUNK = 128
SG_ROWS = 512
SG_DG = D_MODEL // SG_GROUPS
MOBA_HEADS = 32
MOBA_DH = D_MODEL // MOBA_HEADS
MOBA_BLOCK = 256
MOBA_TOPK = 3
ROPE_THETA = 10000.0
EPS = 1e-6

LANE = 128
SUBLANES = 8
VMEM_LIMIT = 56 * 1024 * 1024
VMEM_LIMIT_WIDE = 61 * 1024 * 1024
ROW_TILE = 256
PN_GROUPS = 4
PN_PIECE = 16
PN_COLS = 128
MM_TM = 1024
MM_W_BLOCK_BYTES = 8 * 1024 * 1024


def _params(n_grid, vmem_limit=VMEM_LIMIT):
    return pltpu.CompilerParams(
        dimension_semantics=("arbitrary",) * n_grid, vmem_limit_bytes=vmem_limit)


def _rms(x, g):
    return x * lax.rsqrt(jnp.mean(x * x, axis=-1, keepdims=True) + EPS) * g


def _pre_norm_kernel(h_ref, g_ref, xn_ref, hcopy_ref):
    h = h_ref[...]
    xn_ref[...] = _rms(h, g_ref[...]).astype(BF16)
    hcopy_ref[...] = h


def _gain_spec(idx):
    return pl.BlockSpec((None, 1, D_MODEL), lambda *_: (idx, 0, 0))


def pre_norm(h, gains, g_idx):
    m, d = h.shape
    row = pl.BlockSpec((ROW_TILE, d), lambda i: (i, 0))
    return pl.pallas_call(
        _pre_norm_kernel,
        out_shape=(jax.ShapeDtypeStruct((m, d), BF16), jax.ShapeDtypeStruct((m, d), F32)),
        grid=(m // ROW_TILE,),
        in_specs=[row, _gain_spec(g_idx)],
        out_specs=(row, row),
        compiler_params=_params(1),
        name="pre_norm",
    )(h, gains)


def _post_norm_rows(y_ref, h_ref, gpost_ref, gpre_ref, hout_ref, xn_ref, coef):
    d = y_ref.shape[-1]
    col_chunks = [slice(c0, c0 + PN_COLS) for c0 in range(0, d, PN_COLS)]

    def mean_sq(ref):
        total = 0.0
        for cs in col_chunks:
            total = total + jnp.sum(jnp.square(ref[:, cs].astype(F32)), axis=-1, keepdims=True)
        return total * (1.0 / d)

    rs = lax.rsqrt(mean_sq(y_ref) + EPS)
    for cs in col_chunks:
        hout_ref[:, cs] = h_ref[:, cs] + coef * (y_ref[:, cs].astype(F32) * rs * gpost_ref[:, cs])
    if xn_ref is not None:
        rs = lax.rsqrt(mean_sq(hout_ref) + EPS)
        for cs in col_chunks:
            xn_ref[:, cs] = (hout_ref[:, cs] * rs * gpre_ref[:, cs]).astype(BF16)


def _split_norm_refs(refs, has_pre):
    gpost_ref = refs[0]
    gpre_ref = refs[1] if has_pre else None
    return gpost_ref, gpre_ref, refs[1 + 2 * has_pre:]


def _post_norm_kernel(y_ref, h_ref, *refs, coef, has_pre):
    gpost_ref, gpre_ref, outs = _split_norm_refs(refs, has_pre)
    _post_norm_rows(y_ref, h_ref, gpost_ref, gpre_ref, outs[0], outs[1] if has_pre else None, coef)


def _mm_kernel(*refs, n_parts, n_extra, epilogue):
    x_ref = refs[0]
    w_refs = refs[1:1 + n_parts]
    extra_refs = refs[1 + n_parts:1 + n_parts + n_extra]
    out_ref = refs[1 + n_parts + n_extra]
    x = x_ref[...]
    accs = [jnp.dot(x, w_ref[...].astype(BF16), preferred_element_type=F32) for w_ref in w_refs]
    out_ref[...] = epilogue(accs, extra_refs).astype(out_ref.dtype)


def _mm_col_tile(k, n_parts, budget=MM_W_BLOCK_BYTES):
    tn = LANE
    while n_parts * k * (2 * tn) * 4 <= budget:
        tn *= 2
    return tn


def matmul(x, w, lead, *, col_offsets, n_cols, epilogue, out_dtype, row_extras=(), name,
           row_tiles=None, wide=False):
    m, k = x.shape
    tm = MM_TM
    tn = _mm_col_tile(k, len(col_offsets), MM_W_BLOCK_BYTES * (2 if wide else 1))
    assert m % tm == 0 and n_cols % tn == 0 and all(o % tn == 0 for o in col_offsets)
    assert w.shape[len(lead)] == k
    row0, n_row = (0, m // tm) if row_tiles is None else row_tiles
    x_mode = dict(pipeline_mode=pl.Buffered(1)) if wide else {}
    in_specs = [pl.BlockSpec((tm, k), lambda i, j: (row0 + i, 0), **x_mode)]
    w_block = (None,) * len(lead) + (k, tn)
    for off in col_offsets:
        in_specs.append(pl.BlockSpec(w_block, lambda i, j, ob=off // tn: (*lead, 0, ob + j)))
    for e in row_extras:
        in_specs.append(pl.BlockSpec((tm, e.shape[1]), lambda i, j: (row0 + i, 0)))
    kern = functools.partial(_mm_kernel, n_parts=len(col_offsets), n_extra=len(row_extras),
                             epilogue=epilogue)
    return pl.pallas_call(
        kern,
        out_shape=jax.ShapeDtypeStruct((n_row * tm, n_cols), out_dtype),
        grid=(n_row, n_cols // tn),
        in_specs=in_specs,
        out_specs=pl.BlockSpec((tm, tn), lambda i, j: (i, j)),
        compiler_params=_params(2, VMEM_LIMIT_WIDE if wide else VMEM_LIMIT),
        name=name,
    )(x, *([w] * len(col_offsets)), *row_extras)


def _mm_post_kernel(x_ref, w_ref, yprev_ref, h_ref, *refs, coef, has_pre):
    gpost_ref, gpre_ref, outs = _split_norm_refs(refs, has_pre)
    for r in range(0, h_ref.shape[0], PN_PIECE):
        rows = slice(r, r + PN_PIECE)
        _post_norm_rows(yprev_ref.at[rows], h_ref.at[rows], gpost_ref, gpre_ref, outs[1].at[rows],
                        outs[2].at[rows] if has_pre else None, coef)
    y_ref = outs[0]
    y_ref[...] = jnp.dot(x_ref[...], w_ref[...].astype(BF16),
                         preferred_element_type=F32).astype(y_ref.dtype)


def out_proj_post_norm(x, w, lead, h, xn, gains, post_idx, pre_idx, coef, name):
    m, k = x.shape
    d = D_MODEL
    tm = MM_TM
    tn = _mm_col_tile(k, 1)
    n_col = d // tn
    group_rows = m // PN_GROUPS
    tiles = group_rows // tm
    steps = tiles * n_col
    chunk = group_rows // steps
    assert group_rows % tm == 0 and group_rows % steps == 0 and chunk % PN_PIECE == 0
    assert group_rows % ROW_TILE == 0 and w.shape[len(lead)] == k and d % PN_COLS == 0
    has_pre = pre_idx is not None
    gain_specs = [_gain_spec(post_idx)] + ([_gain_spec(pre_idx)] if has_pre else [])
    gain_args = [gains] * len(gain_specs)
    xn_spec = [pl.BlockSpec(memory_space=pl.ANY)] * has_pre
    xn_arg = [xn] * has_pre
    w_block = (None,) * len(lead) + (k, tn)

    def norm_outputs(block, index_map):
        shapes = [jax.ShapeDtypeStruct((m, d), F32)] + [jax.ShapeDtypeStruct((m, d), BF16)] * has_pre
        return shapes, [pl.BlockSpec(block, index_map)] * len(shapes)

    y_prev = matmul(x, w, lead, col_offsets=(0,), n_cols=d, epilogue=_ep_plain, out_dtype=BF16,
                    name=name, row_tiles=(0, tiles))
    for q in range(1, PN_GROUPS):
        chunk_map = lambda i, j, q=q: ((q - 1) * steps + i * n_col + j, 0)
        in_specs = [pl.BlockSpec((tm, k), lambda i, j, q=q: (q * tiles + i, 0)),
                    pl.BlockSpec(w_block, lambda i, j: (*lead, 0, j)),
                    pl.BlockSpec((chunk, d), lambda i, j: (i * n_col + j, 0)),
                    pl.BlockSpec((chunk, d), chunk_map)] + gain_specs + xn_spec
        norm_shapes, norm_specs = norm_outputs((chunk, d), chunk_map)
        aliases = {3: 1, **({len(in_specs) - 1: 2} if has_pre else {})}
        outs = pl.pallas_call(
            functools.partial(_mm_post_kernel, coef=coef, has_pre=has_pre),
            out_shape=[jax.ShapeDtypeStruct((group_rows, d), BF16)] + norm_shapes,
            grid=(tiles, n_col),
            in_specs=in_specs,
            out_specs=[pl.BlockSpec((tm, tn), lambda i, j: (i, j))] + norm_specs,
            input_output_aliases=aliases,
            compiler_params=_params(2),
            name=name + "_post",
        )(x, w, y_prev, h, *gain_args, *xn_arg)
        y_prev, h = outs[0], outs[1]
        xn_arg = [outs[2]] if has_pre else []

    first_row = (PN_GROUPS - 1) * (group_rows // ROW_TILE)
    row_map = lambda i: (first_row + i, 0)
    in_specs = [pl.BlockSpec((ROW_TILE, d), lambda i: (i, 0)),
                pl.BlockSpec((ROW_TILE, d), row_map)] + gain_specs + xn_spec
    norm_shapes, norm_specs = norm_outputs((ROW_TILE, d), row_map)
    aliases = {1: 0, **({len(in_specs) - 1: 1} if has_pre else {})}
    outs = pl.pallas_call(
        functools.partial(_post_norm_kernel, coef=coef, has_pre=has_pre),
        out_shape=norm_shapes,
        grid=(group_rows // ROW_TILE,),
        in_specs=in_specs,
        out_specs=norm_specs,
        input_output_aliases=aliases,
        compiler_params=_params(1),
        name="post_norm_tail",
    )(y_prev, h, *gain_args, *xn_arg)
    return outs[0], (outs[1] if has_pre else None)


def _ep_plain(accs, extras):
    return accs[0]


def _ep_swiglu(accs, extras):
    a, b = accs
    return a * jax.nn.sigmoid(a) * b


def _ep_silu(accs, extras):
    a = accs[0]
    return a * jax.nn.sigmoid(a)


def _ep_gelu(accs, extras):
    a = accs[0]
    return 0.5 * a * (1.0 + lax.erf(a * math.sqrt(0.5)))


def _ep_rotary_split(accs, extras, *, head_dim, scale):
    acc = accs[0]
    cos = extras[0][...]
    sin = extras[1][...]
    half = head_dim // 2
    outs = []
    for h0 in range(0, acc.shape[1], head_dim):
        t1 = acc[:, h0:h0 + half]
        t2 = acc[:, h0 + half:h0 + head_dim]
        outs.append(t1 * cos - t2 * sin)
        outs.append(t1 * sin + t2 * cos)
    out = jnp.concatenate(outs, axis=1)
    return out * scale if scale != 1.0 else out


def _ep_rotary_roll(accs, extras, *, head_dim, scale):
    acc = accs[0]
    cosf = extras[0][...]
    sins = extras[1][...]
    outs = []
    for h0 in range(0, acc.shape[1], head_dim):
        t = acc[:, h0:h0 + head_dim]
        outs.append(t * cosf + pltpu.roll(t, head_dim // 2, axis=1) * sins)
    out = jnp.concatenate(outs, axis=1)
    return out * scale if scale != 1.0 else out


def ffn_hidden(xn, w_in, lead):
    return matmul(xn, w_in, lead, col_offsets=(0, D_FF), n_cols=D_FF,
                  epilogue=_ep_swiglu, out_dtype=BF16, name="ffn_in", wide=True)


_NT = (((1,), (1,)), ((), ()))
_TN = (((0,), (0,)), ((), ()))


def _retention_kernel(lg_ref, q_ref, k_ref, v_ref, sg_ref, gn_ref, o_ref, state_ref, *, seq):
    c = RET_CHUNK
    dk = RET_DK
    lg = lg_ref[0]
    row = lax.broadcasted_iota(jnp.int32, (c, c), 0)
    col = lax.broadcasted_iota(jnp.int32, (c, c), 1)
    dist = (row - col).astype(F32)
    intra_decay = jnp.where(dist >= 0, jnp.exp(lg[:, :c] * jnp.maximum(dist, 0.0)), 0.0)
    idx = lax.broadcasted_iota(jnp.int32, (c, dk), 0).astype(F32)
    q_decay = jnp.exp(lg * (idx + 1.0))
    k_decay = jnp.exp(lg * (c - 1.0 - idx))
    chunk_decay = jnp.exp(lg * float(c))
    gn = gn_ref[...]
    state_ref[...] = jnp.zeros_like(state_ref)

    def front(rows):
        qc = q_ref[rows, :]
        kc = k_ref[rows, :]
        vc = v_ref[rows, :]
        scores = lax.dot_general(qc, kc, _NT, preferred_element_type=F32) * intra_decay
        intra = jnp.dot(scores.astype(BF16), vc, preferred_element_type=F32)
        qd = (qc.astype(F32) * q_decay).astype(BF16)
        kd = (kc.astype(F32) * k_decay).astype(BF16)
        update = lax.dot_general(kd, vc, _TN, preferred_element_type=F32)
        return qd, intra, update

    def back(rows, qd, intra, update):
        state = state_ref[...]
        cross = jnp.dot(qd, state.astype(BF16), preferred_element_type=F32)
        state_ref[...] = state * chunk_decay + update
        o = intra + cross
        mu = jnp.mean(o, axis=-1, keepdims=True)
        var = jnp.mean(jnp.square(o - mu), axis=-1, keepdims=True)
        on = (o - mu) * lax.rsqrt(var + EPS) * gn
        o_ref[rows, :] = (sg_ref[rows, :].astype(F32) * on).astype(o_ref.dtype)

    def group(gi, carry):
        rows = lambda u: pl.ds(pl.multiple_of((gi * RET_UNROLL + u) * c, c), c)
        pending = front(rows(0))
        for u in range(RET_UNROLL):
            current = pending
            if u + 1 < RET_UNROLL:
                pending = front(rows(u + 1))
            back(rows(u), *current)
        return carry

    lax.fori_loop(0, seq // (c * RET_UNROLL), group, 0)


def retention_core(q, k, v, sg, gn_g, j, batch, seq):
    m, d = q.shape
    log_gamma = jnp.log1p(-jnp.exp2(-5.0 - jnp.arange(RET_HEADS, dtype=F32)))
    lg = jnp.broadcast_to(log_gamma[:, None, None], (RET_HEADS, 1, RET_DK))
    blk = pl.BlockSpec((seq, RET_DK), lambda b, h: (b, h))
    return pl.pallas_call(
        functools.partial(_retention_kernel, seq=seq),
        out_shape=jax.ShapeDtypeStruct((m, d), BF16),
        grid=(batch, RET_HEADS),
        in_specs=[pl.BlockSpec((1, 1, RET_DK), lambda b, h: (h, 0, 0)),
                  blk, blk, blk, blk,
                  pl.BlockSpec((None, 1, RET_DK), lambda b, h: (j, 0, h))],
        out_specs=blk,
        scratch_shapes=[pltpu.VMEM((RET_DK, RET_DK), F32)],
        compiler_params=_params(2),
        name="retention_core",
    )(lg, q, k, v, sg, gn_g.reshape(gn_g.shape[0], 1, d))


def retention_mixer(xn, cos, sin, w_in, gn_g, j, batch, seq):
    d = D_MODEL
    rot = lambda scale: functools.partial(_ep_rotary_split, head_dim=RET_DK, scale=scale)
    q = matmul(xn, w_in, (j,), col_offsets=(0,), n_cols=d, epilogue=rot(1.0),
               out_dtype=BF16, row_extras=(cos, sin), name="ret_q")
    k = matmul(xn, w_in, (j,), col_offsets=(d,), n_cols=d, epilogue=rot(RET_DK ** -0.5),
               out_dtype=BF16, row_extras=(cos, sin), name="ret_k")
    v = matmul(xn, w_in, (j,), col_offsets=(2 * d,), n_cols=d, epilogue=_ep_plain,
               out_dtype=BF16, name="ret_v")
    sg = matmul(xn, w_in, (j,), col_offsets=(3 * d,), n_cols=d, epilogue=_ep_silu,
                out_dtype=BF16, name="ret_g")
    return retention_core(q, k, v, sg, gn_g, j, batch, seq)


def _sgu_kernel(u_ref, v_ref, lng_ref, ws_ref, bs_ref, o_ref, wm_ref):
    c = SG_CHUNK
    row = lax.broadcasted_iota(jnp.int32, (c, c), 0)
    col = lax.broadcasted_iota(jnp.int32, (c, c), 1)
    causal = row >= col
    for g in range(SG_GROUPS):
        wm_ref[g] = jnp.where(causal, ws_ref[g], 0.0).astype(wm_ref.dtype)
    bs = bs_ref[...]
    lng = lng_ref[...]

    def chunk(n, carry):
        rows = pl.ds(pl.multiple_of(n * c, c), c)
        v = v_ref[rows, :].astype(F32)
        mu = jnp.mean(v, axis=-1, keepdims=True)
        var = jnp.mean(jnp.square(v - mu), axis=-1, keepdims=True)
        vn = ((v - mu) * lax.rsqrt(var + EPS) * lng).astype(BF16)
        for g in range(SG_GROUPS):
            cols = slice(g * SG_DG, (g + 1) * SG_DG)
            mixed = jnp.dot(wm_ref[g], vn[:, cols], preferred_element_type=F32) + bs[:, g:g + 1]
            o_ref[rows, cols] = (u_ref[rows, cols].astype(F32) * mixed).astype(o_ref.dtype)
        return carry

    lax.fori_loop(0, SG_ROWS // c, chunk, 0)


def sgu_core(uv, ln_g, w_s, b_s, j):
    m = uv.shape[0]
    d = D_MODEL
    c = SG_CHUNK
    assert m % SG_ROWS == 0 and SG_ROWS % c == 0
    return pl.pallas_call(
        _sgu_kernel,
        out_shape=jax.ShapeDtypeStruct((m, d), BF16),
        grid=(m // SG_ROWS,),
        in_specs=[pl.BlockSpec((SG_ROWS, d), lambda n: (n, 0)),
                  pl.BlockSpec((SG_ROWS, d), lambda n: (n, 1)),
                  pl.BlockSpec((None, 1, d), lambda n: (j, 0, 0)),
                  pl.BlockSpec((None, SG_GROUPS, c, c), lambda n: (j, 0, 0, 0)),
                  pl.BlockSpec((c, SG_GROUPS), lambda n: (0, 0))],
        out_specs=pl.BlockSpec((SG_ROWS, d), lambda n: (n, 0)),
        scratch_shapes=[pltpu.VMEM((SG_GROUPS, c, c), BF16)],
        compiler_params=_params(1),
        name="sgu_core",
    )(uv, uv, ln_g.reshape(ln_g.shape[0], 1, d), w_s, b_s[j].T)


def spatial_gating_mixer(xn, w_in, ln_g, w_s, b_s, j):
    d = D_MODEL
    uv = matmul(xn, w_in, (j,), col_offsets=(0,), n_cols=2 * d, epilogue=_ep_gelu,
                out_dtype=BF16, name="sg_in")
    return sgu_core(uv, ln_g, w_s, b_s, j)


MOBA_MASK = -3e38


def _moba_kernel(q_ref, k_ref, v_ref, o_ref, kaug_ref, vaug_ref, *, seq):
    blk = MOBA_BLOCK
    dh = MOBA_DH
    nblk = seq // blk
    n_sel = min(MOBA_TOPK, nblk - 1)
    pad_rows = 2 * SUBLANES

    vaug_ref[:, :dh] = v_ref[...]
    vaug_ref[:, dh:] = jnp.ones((seq, dh), vaug_ref.dtype)
    kaug_ref[:, :dh] = k_ref[...]
    key_blk = lax.broadcasted_iota(jnp.int32, (seq, dh), 0) // blk
    key_lane = lax.broadcasted_iota(jnp.int32, (seq, dh), 1)
    kaug_ref[:, dh:] = jnp.where(key_blk == key_lane, 1.0, 0.0).astype(kaug_ref.dtype)

    blk_row = lax.broadcasted_iota(jnp.int32, (pad_rows, seq), 0)
    blk_col = lax.broadcasted_iota(jnp.int32, (pad_rows, seq), 1) // blk
    member = jnp.where(blk_row == blk_col, 1.0, 0.0).astype(BF16)
    k_mean = jnp.dot(member, k_ref[...], preferred_element_type=F32) * (1.0 / blk)
    k_mean_hi = k_mean.astype(BF16)
    k_mean_lo = (k_mean - k_mean_hi.astype(F32)).astype(BF16)
    q_all = q_ref[...]
    gate_t = (lax.dot_general(k_mean_hi, q_all, _NT, preferred_element_type=F32)
              + lax.dot_general(k_mean_lo, q_all, _NT, preferred_element_type=F32))

    row = lax.broadcasted_iota(jnp.int32, (blk, LANE), 0)
    lane = lax.broadcasted_iota(jnp.int32, (blk, LANE), 1)
    blk_id = lax.broadcasted_iota(jnp.int32, (SUBLANES, blk), 0)

    def scores(i):
        q = q_ref[i * blk:(i + 1) * blk, :]
        nk = (i + 1) * blk
        if i > n_sel:
            g = gate_t[:SUBLANES, i * blk:(i + 1) * blk]
            bias = jnp.zeros((SUBLANES, blk), F32)
            for j in range(i):
                gj = g[j:j + 1, :]
                beats = ((g > gj) | ((g == gj) & (blk_id < j))) & (blk_id < i)
                rank = jnp.sum(jnp.where(beats, 1.0, 0.0), axis=0, keepdims=True)
                bias = jnp.where((blk_id == j) & (rank >= float(n_sel)), MOBA_MASK, bias)
            bias_t = jnp.concatenate([bias, jnp.zeros((LANE - SUBLANES, blk), F32)], axis=0)
            q_aug = jnp.concatenate([q, bias_t.T.astype(BF16)], axis=1)
            s = lax.dot_general(q_aug, kaug_ref[0:nk, :], _NT, preferred_element_type=F32)
        else:
            s = lax.dot_general(q, k_ref[0:nk, :], _NT, preferred_element_type=F32)
        chunks = [s[:, c * LANE:(c + 1) * LANE] for c in range(nk // LANE)]
        for t, c in enumerate(range(i * blk // LANE, nk // LANE)):
            chunks[c] = jnp.where(lane + t * LANE <= row, chunks[c], float("-inf"))
        return chunks

    def attend(i, chunks):
        nk = (i + 1) * blk
        m_el = chunks[0]
        for ch in chunks[1:]:
            m_el = jnp.maximum(m_el, ch)
        m = m_el.max(axis=-1, keepdims=True)
        p = jnp.concatenate([jnp.exp(ch - m).astype(BF16) for ch in chunks], axis=1)
        o_aug = jnp.dot(p, vaug_ref[0:nk, :], preferred_element_type=F32)
        o_ref[i * blk:(i + 1) * blk, :] = (o_aug[:, :dh] / o_aug[:, dh:]).astype(o_ref.dtype)

    pending = scores(0)
    for i in range(nblk):
        current = pending
        if i + 1 < nblk:
            pending = scores(i + 1)
        attend(i, current)


def moba_core(q, k, v, batch, seq):
    m, d = q.shape
    assert seq % MOBA_BLOCK == 0 and MOBA_DH == LANE and seq // MOBA_BLOCK <= SUBLANES
    blk = pl.BlockSpec((seq, MOBA_DH), lambda b, h: (b, h))
    return pl.pallas_call(
        functools.partial(_moba_kernel, seq=seq),
        out_shape=jax.ShapeDtypeStruct((m, d), BF16),
        grid=(batch, d // MOBA_DH),
        in_specs=[blk, blk, blk],
        out_specs=blk,
        scratch_shapes=[pltpu.VMEM((seq, 2 * MOBA_DH), BF16),
                        pltpu.VMEM((seq, 2 * MOBA_DH), BF16)],
        compiler_params=_params(2),
        name="moba_core",
    )(q, k, v)


def moba_mixer(xn, cosf, sins, w_in, j, batch, seq):
    d = D_MODEL
    rot = lambda scale: functools.partial(_ep_rotary_roll, head_dim=MOBA_DH, scale=scale)
    q = matmul(xn, w_in, (j,), col_offsets=(0,), n_cols=d, epilogue=rot(MOBA_DH ** -0.5),
               out_dtype=BF16, row_extras=(cosf, sins), name="moba_q")
    k = matmul(xn, w_in, (j,), col_offsets=(d,), n_cols=d, epilogue=rot(1.0),
               out_dtype=BF16, row_extras=(cosf, sins), name="moba_k")
    v = matmul(xn, w_in, (j,), col_offsets=(2 * d,), n_cols=d, epilogue=_ep_plain,
               out_dtype=BF16, name="moba_v")
    return moba_core(q, k, v, batch, seq)


def kernel(x, positions, norm_g, ffn_w_in, ffn_w_out, ret_w_in, ret_gn_g, ret_w_out,
           sg_w_in, sg_ln_g, sg_w_s, sg_b, sg_w_out, moba_w_in, moba_w_out):
    batch, seq, d = x.shape
    m = batch * seq
    pos = positions.reshape(m, 1).astype(F32)

    ret_inv = 1.0 / (10000.0 ** jnp.linspace(0.0, 1.0, RET_DK // 2, dtype=F32))
    ret_ang = pos * ret_inv
    ret_cos, ret_sin = jnp.cos(ret_ang), jnp.sin(ret_ang)
    moba_inv = 1.0 / (ROPE_THETA ** (jnp.arange(0, MOBA_DH, 2, dtype=F32) / MOBA_DH))
    moba_ang = pos * moba_inv
    mc, ms = jnp.cos(moba_ang), jnp.sin(moba_ang)
    moba_cosf = jnp.concatenate([mc, mc], axis=1)
    moba_sins = jnp.concatenate([-ms, ms], axis=1)

    n_gain = norm_g.shape[1]
    gains = norm_g.reshape(DEPTH * n_gain, 1, d)
    h = x.reshape(m, d)
    xn, h = pre_norm(h, gains, 0)
    for i in range(DEPTH):
        mix, j = i % N_MIXERS, i // N_MIXERS
        g0 = i * n_gain
        hid = ffn_hidden(xn, ffn_w_in, (i, 0))
        h, xn = out_proj_post_norm(hid, ffn_w_out, (i, 0), h, xn, gains, g0 + 1, g0 + 2, 0.5, "ffn_out")
        if mix == 0:
            o = retention_mixer(xn, ret_cos, ret_sin, ret_w_in, ret_gn_g, j, batch, seq)
            w_out, name = ret_w_out, "ret_out"
        elif mix == 1:
            o = spatial_gating_mixer(xn, sg_w_in, sg_ln_g, sg_w_s, sg_b, j)
            w_out, name = sg_w_out, "sg_out"
        else:
            o = moba_mixer(xn, moba_cosf, moba_sins, moba_w_in, j, batch, seq)
            w_out, name = moba_w_out, "moba_out"
        h, xn = out_proj_post_norm(o, w_out, (j,), h, xn, gains, g0 + 3, g0 + 4, 1.0, name)
        hid = ffn_hidden(xn, ffn_w_in, (i, 1))
        pre_next = g0 + n_gain if i + 1 < DEPTH else None
        h, xn = out_proj_post_norm(hid, ffn_w_out, (i, 1), h, xn, gains, g0 + 5, pre_next, 0.5, "ffn_out")
    return h.reshape(batch, seq, d)
```

```python
import functools
import math

import jax
import jax.numpy as jnp
from jax import lax
from jax.experimental import pallas as pl
from jax.experimental.pallas import tpu as pltpu

F32 = jnp.float32
BF16 = jnp.bfloat16

D_MODEL = 4096
D_FF = 5632
DEPTH = 4
N_MIXERS = 3
RET_HEADS = 16
RET_DK = D_MODEL // RET_HEADS
RET_CHUNK = 128
RET_UNROLL = 16
SG_GROUPS = 16
SG_CHUNK = 128
SG_ROWS = 512
SG_DG = D_MODEL // SG_GROUPS
MOBA_HEADS = 32
MOBA_DH = D_MODEL // MOBA_HEADS
MOBA_BLOCK = 256
MOBA_TOPK = 3
ROPE_THETA = 10000.0
EPS = 1e-6

LANE = 128
SUBLANES = 8
VMEM_LIMIT = 56 * 1024 * 1024
ROW_TILE = 256
PN_GROUPS = 4
PN_PIECE = 16
PN_COLS = 128
MM_TM = 1024
MM_W_BLOCK_BYTES = 8 * 1024 * 1024


def _params(n_grid):
    return pltpu.CompilerParams(
        dimension_semantics=("arbitrary",) * n_grid, vmem_limit_bytes=VMEM_LIMIT)


def _rms(x, g):
    return x * lax.rsqrt(jnp.mean(x * x, axis=-1, keepdims=True) + EPS) * g


def _pre_norm_kernel(h_ref, g_ref, xn_ref, hcopy_ref):
    h = h_ref[...]
    xn_ref[...] = _rms(h, g_ref[...]).astype(BF16)
    hcopy_ref[...] = h


def _gain_spec(idx):
    return pl.BlockSpec((None, 1, D_MODEL), lambda *_: (idx, 0, 0))


def pre_norm(h, gains, g_idx):
    m, d = h.shape
    row = pl.BlockSpec((ROW_TILE, d), lambda i: (i, 0))
    return pl.pallas_call(
        _pre_norm_kernel,
        out_shape=(jax.ShapeDtypeStruct((m, d), BF16), jax.ShapeDtypeStruct((m, d), F32)),
        grid=(m // ROW_TILE,),
        in_specs=[row, _gain_spec(g_idx)],
        out_specs=(row, row),
        compiler_params=_params(1),
        name="pre_norm",
    )(h, gains)


def _post_norm_rows(y_ref, h_ref, gpost_ref, gpre_ref, hout_ref, xn_ref, coef):
    d = y_ref.shape[-1]
    col_chunks = [slice(c0, c0 + PN_COLS) for c0 in range(0, d, PN_COLS)]

    def mean_sq(ref):
        total = 0.0
        for cs in col_chunks:
            total = total + jnp.sum(jnp.square(ref[:, cs].astype(F32)), axis=-1, keepdims=True)
        return total * (1.0 / d)

    rs = lax.rsqrt(mean_sq(y_ref) + EPS)
    for cs in col_chunks:
        hout_ref[:, cs] = h_ref[:, cs] + coef * (y_ref[:, cs].astype(F32) * rs * gpost_ref[:, cs])
    if xn_ref is not None:
        rs = lax.rsqrt(mean_sq(hout_ref) + EPS)
        for cs in col_chunks:
            xn_ref[:, cs] = (hout_ref[:, cs] * rs * gpre_ref[:, cs]).astype(BF16)


def _split_norm_refs(refs, has_pre):
    gpost_ref = refs[0]
    gpre_ref = refs[1] if has_pre else None
    return gpost_ref, gpre_ref, refs[1 + 2 * has_pre:]


def _post_norm_kernel(y_ref, h_ref, *refs, coef, has_pre):
    gpost_ref, gpre_ref, outs = _split_norm_refs(refs, has_pre)
    _post_norm_rows(y_ref, h_ref, gpost_ref, gpre_ref, outs[0], outs[1] if has_pre else None, coef)


def _mm_kernel(*refs, n_parts, n_extra, epilogue):
    x_ref = refs[0]
    w_refs = refs[1:1 + n_parts]
    extra_refs = refs[1 + n_parts:1 + n_parts + n_extra]
    out_ref = refs[1 + n_parts + n_extra]
    x = x_ref[...]
    accs = [jnp.dot(x, w_ref[...].astype(BF16), preferred_element_type=F32) for w_ref in w_refs]
    out_ref[...] = epilogue(accs, extra_refs).astype(out_ref.dtype)


def _mm_col_tile(k, n_parts):
    tn = LANE
    while n_parts * k * (2 * tn) * 4 <= MM_W_BLOCK_BYTES:
        tn *= 2
    return tn


def matmul(x, w, lead, *, col_offsets, n_cols, epilogue, out_dtype, row_extras=(), name,
           row_tiles=None):
    m, k = x.shape
    tm = MM_TM
    tn = _mm_col_tile(k, len(col_offsets))
    assert m % tm == 0 and n_cols % tn == 0 and all(o % tn == 0 for o in col_offsets)
    assert w.shape[len(lead)] == k
    row0, n_row = (0, m // tm) if row_tiles is None else row_tiles
    in_specs = [pl.BlockSpec((tm, k), lambda i, j: (row0 + i, 0))]
    w_block = (None,) * len(lead) + (k, tn)
    for off in col_offsets:
        in_specs.append(pl.BlockSpec(w_block, lambda i, j, ob=off // tn: (*lead, 0, ob + j)))
    for e in row_extras:
        in_specs.append(pl.BlockSpec((tm, e.shape[1]), lambda i, j: (row0 + i, 0)))
    kern = functools.partial(_mm_kernel, n_parts=len(col_offsets), n_extra=len(row_extras),
                             epilogue=epilogue)
    return pl.pallas_call(
        kern,
        out_shape=jax.ShapeDtypeStruct((n_row * tm, n_cols), out_dtype),
        grid=(n_row, n_cols // tn),
        in_specs=in_specs,
        out_specs=pl.BlockSpec((tm, tn), lambda i, j: (i, j)),
        compiler_params=_params(2),
        name=name,
    )(x, *([w] * len(col_offsets)), *row_extras)


def _mm_post_kernel(x_ref, w_ref, yprev_ref, h_ref, *refs, coef, has_pre):
    gpost_ref, gpre_ref, outs = _split_norm_refs(refs, has_pre)
    for r in range(0, h_ref.shape[0], PN_PIECE):
        rows = slice(r, r + PN_PIECE)
        _post_norm_rows(yprev_ref.at[rows], h_ref.at[rows], gpost_ref, gpre_ref, outs[1].at[rows],
                        outs[2].at[rows] if has_pre else None, coef)
    y_ref = outs[0]
    y_ref[...] = jnp.dot(x_ref[...], w_ref[...].astype(BF16),
                         preferred_element_type=F32).astype(y_ref.dtype)


def out_proj_post_norm(x, w, lead, h, xn, gains, post_idx, pre_idx, coef, name, defer_tail=False):
    x_head, x_last = x if isinstance(x, tuple) else (x, None)
    m, k = h.shape[0], x_head.shape[1]
    d = D_MODEL
    tm = MM_TM
    tn = _mm_col_tile(k, 1)
    n_col = d // tn
    group_rows = m // PN_GROUPS
    tiles = group_rows // tm
    steps = tiles * n_col
    chunk = group_rows // steps
    assert group_rows % tm == 0 and group_rows % steps == 0 and chunk % PN_PIECE == 0
    assert group_rows % ROW_TILE == 0 and w.shape[len(lead)] == k and d % PN_COLS == 0
    has_pre = pre_idx is not None
    gain_specs = [_gain_spec(post_idx)] + ([_gain_spec(pre_idx)] if has_pre else [])
    gain_args = [gains] * len(gain_specs)
    xn_spec = [pl.BlockSpec(memory_space=pl.ANY)] * has_pre
    xn_arg = [xn] * has_pre
    w_block = (None,) * len(lead) + (k, tn)

    def norm_outputs(block, index_map):
        shapes = [jax.ShapeDtypeStruct((m, d), F32)] + [jax.ShapeDtypeStruct((m, d), BF16)] * has_pre
        return shapes, [pl.BlockSpec(block, index_map)] * len(shapes)

    y_prev = matmul(x_head, w, lead, col_offsets=(0,), n_cols=d, epilogue=_ep_plain, out_dtype=BF16,
                    name=name, row_tiles=(0, tiles))
    for q in range(1, PN_GROUPS):
        chunk_map = lambda i, j, q=q: ((q - 1) * steps + i * n_col + j, 0)
        separate_last = q == PN_GROUPS - 1 and x_last is not None
        x_q, tile0 = (x_last, 0) if separate_last else (x_head, q * tiles)
        in_specs = [pl.BlockSpec((tm, k), lambda i, j, tile0=tile0: (tile0 + i, 0)),
                    pl.BlockSpec(w_block, lambda i, j: (*lead, 0, j)),
                    pl.BlockSpec((chunk, d), lambda i, j: (i * n_col + j, 0)),
                    pl.BlockSpec((chunk, d), chunk_map)] + gain_specs + xn_spec
        norm_shapes, norm_specs = norm_outputs((chunk, d), chunk_map)
        aliases = {3: 1, **({len(in_specs) - 1: 2} if has_pre else {})}
        outs = pl.pallas_call(
            functools.partial(_mm_post_kernel, coef=coef, has_pre=has_pre),
            out_shape=[jax.ShapeDtypeStruct((group_rows, d), BF16)] + norm_shapes,
            grid=(tiles, n_col),
            in_specs=in_specs,
            out_specs=[pl.BlockSpec((tm, tn), lambda i, j: (i, j))] + norm_specs,
            input_output_aliases=aliases,
            compiler_params=_params(2),
            name=name + "_post",
        )(x_q, w, y_prev, h, *gain_args, *xn_arg)
        y_prev, h = outs[0], outs[1]
        xn_arg = [outs[2]] if has_pre else []

    if defer_tail:
        return h, xn_arg[0], y_prev
    first_row = (PN_GROUPS - 1) * (group_rows // ROW_TILE)
    row_map = lambda i: (first_row + i, 0)
    in_specs = [pl.BlockSpec((ROW_TILE, d), lambda i: (i, 0)),
                pl.BlockSpec((ROW_TILE, d), row_map)] + gain_specs + xn_spec
    norm_shapes, norm_specs = norm_outputs((ROW_TILE, d), row_map)
    aliases = {1: 0, **({len(in_specs) - 1: 1} if has_pre else {})}
    outs = pl.pallas_call(
        functools.partial(_post_norm_kernel, coef=coef, has_pre=has_pre),
        out_shape=norm_shapes,
        grid=(group_rows // ROW_TILE,),
        in_specs=in_specs,
        out_specs=norm_specs,
        input_output_aliases=aliases,
        compiler_params=_params(1),
        name="post_norm_tail",
    )(y_prev, h, *gain_args, *xn_arg)
    return outs[0], (outs[1] if has_pre else None)


def _ep_plain(accs, extras):
    return accs[0]


def _ep_swiglu(accs, extras):
    a, b = accs
    return a * jax.nn.sigmoid(a) * b


def _ep_silu(accs, extras):
    a = accs[0]
    return a * jax.nn.sigmoid(a)


def _ep_gelu(accs, extras):
    a = accs[0]
    return 0.5 * a * (1.0 + lax.erf(a * math.sqrt(0.5)))


def _ep_rotary_split(accs, extras, *, head_dim, scale):
    acc = accs[0]
    cos = extras[0][...]
    sin = extras[1][...]
    half = head_dim // 2
    outs = []
    for h0 in range(0, acc.shape[1], head_dim):
        t1 = acc[:, h0:h0 + half]
        t2 = acc[:, h0 + half:h0 + head_dim]
        outs.append(t1 * cos - t2 * sin)
        outs.append(t1 * sin + t2 * cos)
    out = jnp.concatenate(outs, axis=1)
    return out * scale if scale != 1.0 else out


def _ep_rotary_roll(accs, extras, *, head_dim, scale):
    acc = accs[0]
    cosf = extras[0][...]
    sins = extras[1][...]
    outs = []
    for h0 in range(0, acc.shape[1], head_dim):
        t = acc[:, h0:h0 + head_dim]
        outs.append(t * cosf + pltpu.roll(t, head_dim // 2, axis=1) * sins)
    out = jnp.concatenate(outs, axis=1)
    return out * scale if scale != 1.0 else out


def ffn_hidden(xn, w_in, lead):
    return matmul(xn, w_in, lead, col_offsets=(0, D_FF), n_cols=D_FF,
                  epilogue=_ep_swiglu, out_dtype=BF16, name="ffn_in")


def _ffn_in_tail_kernel(x_ref, wa_ref, wb_ref, ylast_ref, h_ref, gpost_ref, gpre_ref,
                        hid_ref, hout_ref, xn_ref, *, coef):
    _post_norm_rows(ylast_ref, h_ref, gpost_ref, gpre_ref, hout_ref, xn_ref, coef)
    x = x_ref[...]
    accs = [jnp.dot(x, w_ref[...].astype(BF16), preferred_element_type=F32)
            for w_ref in (wa_ref, wb_ref)]
    hid_ref[...] = _ep_swiglu(accs, ()).astype(hid_ref.dtype)


def ffn_hidden_with_tail(xn, h, y_last, w_in, lead, gains, post_idx, pre_idx, coef):
    m, k = xn.shape
    d = D_MODEL
    tm = MM_TM
    tn = _mm_col_tile(k, 2)
    n_col = D_FF // tn
    group_rows = m // PN_GROUPS
    lead_tiles = (m - group_rows) // tm
    n_piece = group_rows // PN_PIECE
    piece0 = (m - group_rows) // PN_PIECE
    surplus = lead_tiles * n_col - n_piece
    assert surplus >= 0 and D_FF % tn == 0 and y_last.shape == (group_rows, d)
    piece = lambda i, j: jnp.maximum(i * n_col + j - surplus, 0)
    w_block = (None,) * len(lead) + (k, tn)
    hid_head, h, xn = pl.pallas_call(
        functools.partial(_ffn_in_tail_kernel, coef=coef),
        out_shape=[jax.ShapeDtypeStruct((lead_tiles * tm, D_FF), BF16),
                   jax.ShapeDtypeStruct((m, d), F32), jax.ShapeDtypeStruct((m, d), BF16)],
        grid=(lead_tiles, n_col),
        in_specs=[pl.BlockSpec((tm, k), lambda i, j: (i, 0)),
                  pl.BlockSpec(w_block, lambda i, j: (*lead, 0, j)),
                  pl.BlockSpec(w_block, lambda i, j: (*lead, 0, n_col + j)),
                  pl.BlockSpec((PN_PIECE, d), lambda i, j: (piece(i, j), 0)),
                  pl.BlockSpec((PN_PIECE, d), lambda i, j: (piece0 + piece(i, j), 0)),
                  _gain_spec(post_idx), _gain_spec(pre_idx)],
        out_specs=[pl.BlockSpec((tm, tn), lambda i, j: (i, j)),
                   pl.BlockSpec((PN_PIECE, d), lambda i, j: (piece0 + piece(i, j), 0)),
                   pl.BlockSpec((PN_PIECE, d), lambda i, j: (piece0 + piece(i, j), 0))],
        input_output_aliases={4: 1, 0: 2},
        compiler_params=_params(2),
        name="ffn_in_tail",
    )(xn, w_in, w_in, y_last, h, gains, gains)
    hid_last = matmul(xn, w_in, lead, col_offsets=(0, D_FF), n_cols=D_FF, epilogue=_ep_swiglu,
                      out_dtype=BF16, name="ffn_in", row_tiles=(lead_tiles, group_rows // tm))
    return hid_head, hid_last, h, xn


_NT = (((1,), (1,)), ((), ()))
_TN = (((0,), (0,)), ((), ()))


def _retention_kernel(lg_ref, q_ref, k_ref, v_ref, sg_ref, gn_ref, o_ref, state_ref, *, seq):
    c = RET_CHUNK
    dk = RET_DK
    lg = lg_ref[0]
    row = lax.broadcasted_iota(jnp.int32, (c, c), 0)
    col = lax.broadcasted_iota(jnp.int32, (c, c), 1)
    dist = (row - col).astype(F32)
    intra_decay = jnp.where(dist >= 0, jnp.exp(lg[:, :c] * jnp.maximum(dist, 0.0)), 0.0)
    idx = lax.broadcasted_iota(jnp.int32, (c, dk), 0).astype(F32)
    q_decay = jnp.exp(lg * (idx + 1.0))
    k_decay = jnp.exp(lg * (c - 1.0 - idx))
    chunk_decay = jnp.exp(lg * float(c))
    gn = gn_ref[...]
    state_ref[...] = jnp.zeros_like(state_ref)

    def front(rows):
        qc = q_ref[rows, :]
        kc = k_ref[rows, :]
        vc = v_ref[rows, :]
        scores = lax.dot_general(qc, kc, _NT, preferred_element_type=F32) * intra_decay
        intra = jnp.dot(scores.astype(BF16), vc, preferred_element_type=F32)
        qd = (qc.astype(F32) * q_decay).astype(BF16)
        kd = (kc.astype(F32) * k_decay).astype(BF16)
        update = lax.dot_general(kd, vc, _TN, preferred_element_type=F32)
        return qd, intra, update

    def back(rows, qd, intra, update):
        state = state_ref[...]
        cross = jnp.dot(qd, state.astype(BF16), preferred_element_type=F32)
        state_ref[...] = state * chunk_decay + update
        o = intra + cross
        mu = jnp.mean(o, axis=-1, keepdims=True)
        var = jnp.mean(jnp.square(o - mu), axis=-1, keepdims=True)
        on = (o - mu) * lax.rsqrt(var + EPS) * gn
        o_ref[rows, :] = (sg_ref[rows, :].astype(F32) * on).astype(o_ref.dtype)

    def group(gi, carry):
        rows = lambda u: pl.ds(pl.multiple_of((gi * RET_UNROLL + u) * c, c), c)
        pending = front(rows(0))
        for u in range(RET_UNROLL):
            current = pending
            if u + 1 < RET_UNROLL:
                pending = front(rows(u + 1))
            back(rows(u), *current)
        return carry

    lax.fori_loop(0, seq // (c * RET_UNROLL), group, 0)


def retention_core(q, k, v, sg, gn_g, j, batch, seq):
    m, d = q.shape
    log_gamma = jnp.log1p(-jnp.exp2(-5.0 - jnp.arange(RET_HEADS, dtype=F32)))
    lg = jnp.broadcast_to(log_gamma[:, None, None], (RET_HEADS, 1, RET_DK))
    blk = pl.BlockSpec((seq, RET_DK), lambda b, h: (b, h))
    return pl.pallas_call(
        functools.partial(_retention_kernel, seq=seq),
        out_shape=jax.ShapeDtypeStruct((m, d), BF16),
        grid=(batch, RET_HEADS),
        in_specs=[pl.BlockSpec((1, 1, RET_DK), lambda b, h: (h, 0, 0)),
                  blk, blk, blk, blk,
                  pl.BlockSpec((None, 1, RET_DK), lambda b, h: (j, 0, h))],
        out_specs=blk,
        scratch_shapes=[pltpu.VMEM((RET_DK, RET_DK), F32)],
        compiler_params=_params(2),
        name="retention_core",
    )(lg, q, k, v, sg, gn_g.reshape(gn_g.shape[0], 1, d))


def retention_mixer(xn, cos, sin, w_in, gn_g, j, batch, seq):
    d = D_MODEL
    rot = lambda scale: functools.partial(_ep_rotary_split, head_dim=RET_DK, scale=scale)
    q = matmul(xn, w_in, (j,), col_offsets=(0,), n_cols=d, epilogue=rot(1.0),
               out_dtype=BF16, row_extras=(cos, sin), name="ret_q")
    k = matmul(xn, w_in, (j,), col_offsets=(d,), n_cols=d, epilogue=rot(RET_DK ** -0.5),
               out_dtype=BF16, row_extras=(cos, sin), name="ret_k")
    v = matmul(xn, w_in, (j,), col_offsets=(2 * d,), n_cols=d, epilogue=_ep_plain,
               out_dtype=BF16, name="ret_v")
    sg = matmul(xn, w_in, (j,), col_offsets=(3 * d,), n_cols=d, epilogue=_ep_silu,
                out_dtype=BF16, name="ret_g")
    return retention_core(q, k, v, sg, gn_g, j, batch, seq)


def _sgu_kernel(u_ref, v_ref, lng_ref, ws_ref, bs_ref, o_ref, wm_ref):
    c = SG_CHUNK
    row = lax.broadcasted_iota(jnp.int32, (c, c), 0)
    col = lax.broadcasted_iota(jnp.int32, (c, c), 1)
    causal = row >= col
    for g in range(SG_GROUPS):
        wm_ref[g] = jnp.where(causal, ws_ref[g], 0.0).astype(wm_ref.dtype)
    bs = bs_ref[...]
    lng = lng_ref[...]

    def chunk(n, carry):
        rows = pl.ds(pl.multiple_of(n * c, c), c)
        v = v_ref[rows, :].astype(F32)
        mu = jnp.mean(v, axis=-1, keepdims=True)
        var = jnp.mean(jnp.square(v - mu), axis=-1, keepdims=True)
        vn = ((v - mu) * lax.rsqrt(var + EPS) * lng).astype(BF16)
        for g in range(SG_GROUPS):
            cols = slice(g * SG_DG, (g + 1) * SG_DG)
            mixed = jnp.dot(wm_ref[g], vn[:, cols], preferred_element_type=F32) + bs[:, g:g + 1]
            o_ref[rows, cols] = (u_ref[rows, cols].astype(F32) * mixed).astype(o_ref.dtype)
        return carry

    lax.fori_loop(0, SG_ROWS // c, chunk, 0)


def sgu_core(uv, ln_g, w_s, b_s, j):
    m = uv.shape[0]
    d = D_MODEL
    c = SG_CHUNK
    assert m % SG_ROWS == 0 and SG_ROWS % c == 0
    return pl.pallas_call(
        _sgu_kernel,
        out_shape=jax.ShapeDtypeStruct((m, d), BF16),
        grid=(m // SG_ROWS,),
        in_specs=[pl.BlockSpec((SG_ROWS, d), lambda n: (n, 0)),
                  pl.BlockSpec((SG_ROWS, d), lambda n: (n, 1)),
                  pl.BlockSpec((None, 1, d), lambda n: (j, 0, 0)),
                  pl.BlockSpec((None, SG_GROUPS, c, c), lambda n: (j, 0, 0, 0)),
                  pl.BlockSpec((c, SG_GROUPS), lambda n: (0, 0))],
        out_specs=pl.BlockSpec((SG_ROWS, d), lambda n: (n, 0)),
        scratch_shapes=[pltpu.VMEM((SG_GROUPS, c, c), BF16)],
        compiler_params=_params(1),
        name="sgu_core",
    )(uv, uv, ln_g.reshape(ln_g.shape[0], 1, d), w_s, b_s[j].T)


def spatial_gating_mixer(xn, w_in, ln_g, w_s, b_s, j):
    d = D_MODEL
    uv = matmul(xn, w_in, (j,), col_offsets=(0,), n_cols=2 * d, epilogue=_ep_gelu,
                out_dtype=BF16, name="sg_in")
    return sgu_core(uv, ln_g, w_s, b_s, j)


MOBA_MASK = -3e38


def _moba_kernel(q_ref, k_ref, v_ref, o_ref, kaug_ref, vaug_ref, *, seq):
    blk = MOBA_BLOCK
    dh = MOBA_DH
    nblk = seq // blk
    n_sel = min(MOBA_TOPK, nblk - 1)
    pad_rows = 2 * SUBLANES

    vaug_ref[:, :dh] = v_ref[...]
    vaug_ref[:, dh:] = jnp.ones((seq, dh), vaug_ref.dtype)
    kaug_ref[:, :dh] = k_ref[...]
    key_blk = lax.broadcasted_iota(jnp.int32, (seq, dh), 0) // blk
    key_lane = lax.broadcasted_iota(jnp.int32, (seq, dh), 1)
    kaug_ref[:, dh:] = jnp.where(key_blk == key_lane, 1.0, 0.0).astype(kaug_ref.dtype)

    blk_row = lax.broadcasted_iota(jnp.int32, (pad_rows, seq), 0)
    blk_col = lax.broadcasted_iota(jnp.int32, (pad_rows, seq), 1) // blk
    member = jnp.where(blk_row == blk_col, 1.0, 0.0).astype(BF16)
    k_mean = jnp.dot(member, k_ref[...], preferred_element_type=F32) * (1.0 / blk)
    k_mean_hi = k_mean.astype(BF16)
    k_mean_lo = (k_mean - k_mean_hi.astype(F32)).astype(BF16)
    q_all = q_ref[...]
    gate_t = (lax.dot_general(k_mean_hi, q_all, _NT, preferred_element_type=F32)
              + lax.dot_general(k_mean_lo, q_all, _NT, preferred_element_type=F32))

    row = lax.broadcasted_iota(jnp.int32, (blk, LANE), 0)
    lane = lax.broadcasted_iota(jnp.int32, (blk, LANE), 1)
    blk_id = lax.broadcasted_iota(jnp.int32, (SUBLANES, blk), 0)

    def scores(i):
        q = q_ref[i * blk:(i + 1) * blk, :]
        nk = (i + 1) * blk
        if i > n_sel:
            g = gate_t[:SUBLANES, i * blk:(i + 1) * blk]
            bias = jnp.zeros((SUBLANES, blk), F32)
            for j in range(i):
                gj = g[j:j + 1, :]
                beats = ((g > gj) | ((g == gj) & (blk_id < j))) & (blk_id < i)
                rank = jnp.sum(jnp.where(beats, 1.0, 0.0), axis=0, keepdims=True)
                bias = jnp.where((blk_id == j) & (rank >= float(n_sel)), MOBA_MASK, bias)
            bias_t = jnp.concatenate([bias, jnp.zeros((LANE - SUBLANES, blk), F32)], axis=0)
            q_aug = jnp.concatenate([q, bias_t.T.astype(BF16)], axis=1)
            s = lax.dot_general(q_aug, kaug_ref[0:nk, :], _NT, preferred_element_type=F32)
        else:
            s = lax.dot_general(q, k_ref[0:nk, :], _NT, preferred_element_type=F32)
        chunks = [s[:, c * LANE:(c + 1) * LANE] for c in range(nk // LANE)]
        for t, c in enumerate(range(i * blk // LANE, nk // LANE)):
            chunks[c] = jnp.where(lane + t * LANE <= row, chunks[c], float("-inf"))
        return chunks

    def attend(i, chunks):
        nk = (i + 1) * blk
        m_el = chunks[0]
        for ch in chunks[1:]:
            m_el = jnp.maximum(m_el, ch)
        m = m_el.max(axis=-1, keepdims=True)
        p = jnp.concatenate([jnp.exp(ch - m).astype(BF16) for ch in chunks], axis=1)
        o_aug = jnp.dot(p, vaug_ref[0:nk, :], preferred_element_type=F32)
        o_ref[i * blk:(i + 1) * blk, :] = (o_aug[:, :dh] / o_aug[:, dh:]).astype(o_ref.dtype)

    pending = scores(0)
    for i in range(nblk):
        current = pending
        if i + 1 < nblk:
            pending = scores(i + 1)
        attend(i, current)


def moba_core(q, k, v, batch, seq):
    m, d = q.shape
    assert seq % MOBA_BLOCK == 0 and MOBA_DH == LANE and seq // MOBA_BLOCK <= SUBLANES
    blk = pl.BlockSpec((seq, MOBA_DH), lambda b, h: (b, h))
    return pl.pallas_call(
        functools.partial(_moba_kernel, seq=seq),
        out_shape=jax.ShapeDtypeStruct((m, d), BF16),
        grid=(batch, d // MOBA_DH),
        in_specs=[blk, blk, blk],
        out_specs=blk,
        scratch_shapes=[pltpu.VMEM((seq, 2 * MOBA_DH), BF16),
                        pltpu.VMEM((seq, 2 * MOBA_DH), BF16)],
        compiler_params=_params(2),
        name="moba_core",
    )(q, k, v)


def moba_mixer(xn, cosf, sins, w_in, j, batch, seq):
    d = D_MODEL
    rot = lambda scale: functools.partial(_ep_rotary_roll, head_dim=MOBA_DH, scale=scale)
    q = matmul(xn, w_in, (j,), col_offsets=(0,), n_cols=d, epilogue=rot(MOBA_DH ** -0.5),
               out_dtype=BF16, row_extras=(cosf, sins), name="moba_q")
    k = matmul(xn, w_in, (j,), col_offsets=(d,), n_cols=d, epilogue=rot(1.0),
               out_dtype=BF16, row_extras=(cosf, sins), name="moba_k")
    v = matmul(xn, w_in, (j,), col_offsets=(2 * d,), n_cols=d, epilogue=_ep_plain,
               out_dtype=BF16, name="moba_v")
    return moba_core(q, k, v, batch, seq)


def kernel(x, positions, norm_g, ffn_w_in, ffn_w_out, ret_w_in, ret_gn_g, ret_w_out,
           sg_w_in, sg_ln_g, sg_w_s, sg_b, sg_w_out, moba_w_in, moba_w_out):
    batch, seq, d = x.shape
    m = batch * seq
    pos = positions.reshape(m, 1).astype(F32)

    ret_inv = 1.0 / (10000.0 ** jnp.linspace(0.0, 1.0, RET_DK // 2, dtype=F32))
    ret_ang = pos * ret_inv
    ret_cos, ret_sin = jnp.cos(ret_ang), jnp.sin(ret_ang)
    moba_inv = 1.0 / (ROPE_THETA ** (jnp.arange(0, MOBA_DH, 2, dtype=F32) / MOBA_DH))
    moba_ang = pos * moba_inv
    mc, ms = jnp.cos(moba_ang), jnp.sin(moba_ang)
    moba_cosf = jnp.concatenate([mc, mc], axis=1)
    moba_sins = jnp.concatenate([-ms, ms], axis=1)

    n_gain = norm_g.shape[1]
    gains = norm_g.reshape(DEPTH * n_gain, 1, d)
    h = x.reshape(m, d)
    xn, h = pre_norm(h, gains, 0)
    pending = None
    for i in range(DEPTH):
        mix, j = i % N_MIXERS, i // N_MIXERS
        g0 = i * n_gain
        if pending is None:
            hid = ffn_hidden(xn, ffn_w_in, (i, 0))
        else:
            *hid, h, xn = ffn_hidden_with_tail(xn, h, pending[0], ffn_w_in, (i, 0), gains, *pending[1:])
            hid = tuple(hid)
        h, xn = out_proj_post_norm(hid, ffn_w_out, (i, 0), h, xn, gains, g0 + 1, g0 + 2, 0.5, "ffn_out")
        if mix == 0:
            o = retention_mixer(xn, ret_cos, ret_sin, ret_w_in, ret_gn_g, j, batch, seq)
            w_out, name = ret_w_out, "ret_out"
        elif mix == 1:
            o = spatial_gating_mixer(xn, sg_w_in, sg_ln_g, sg_w_s, sg_b, j)
            w_out, name = sg_w_out, "sg_out"
        else:
            o = moba_mixer(xn, moba_cosf, moba_sins, moba_w_in, j, batch, seq)
            w_out, name = moba_w_out, "moba_out"
        h, xn, y_last = out_proj_post_norm(o, w_out, (j,), h, xn, gains, g0 + 3, g0 + 4, 1.0, name,
                                           defer_tail=True)
        *hid, h, xn = ffn_hidden_with_tail(xn, h, y_last, ffn_w_in, (i, 1), gains, g0 + 3, g0 + 4, 1.0)
        if i + 1 < DEPTH:
            h, xn, y_last = out_proj_post_norm(tuple(hid), ffn_w_out, (i, 1), h, xn, gains, g0 + 5,
                                               g0 + n_gain, 0.5, "ffn_out", defer_tail=True)
            pending = (y_last, g0 + 5, g0 + n_gain, 0.5)
        else:
            h, xn = out_proj_post_norm(tuple(hid), ffn_w_out, (i, 1), h, xn, gains, g0 + 5, None, 0.5,
                                       "ffn_out")
    return h.reshape(batch, seq, d)
```
